```python
import math
import jax
import jax.numpy as jnp
from jax import lax
import numpy as np

D_MODEL = 2048
BATCH = 1
SEQ = 8192
DEPTH = 1

NORM_EPS = 1e-6

ATT_HEAD_DIM = 128
ATT_HEADS = D_MODEL // ATT_HEAD_DIM
ATT_WIDTH = ATT_HEADS * ATT_HEAD_DIM
ATT_PATTERNS = ((128, 1), (512, 4), (2048, 16))
ATT_GROUPS = len(ATT_PATTERNS)
ALIBI_MAX_BIAS = 8.0

GDN_HEAD_DIM = 128
GDN_QK_HEADS = D_MODEL // 128
GDN_V_HEADS = 2 * GDN_QK_HEADS
GDN_QK_WIDTH = GDN_QK_HEADS * GDN_HEAD_DIM
GDN_V_WIDTH = GDN_V_HEADS * GDN_HEAD_DIM
GDN_CONV_CH = 2 * GDN_QK_WIDTH + GDN_V_WIDTH
GDN_CONV_WIDTH = 5
GDN_CHUNK = 64

MOE_GROUPS = 8
MOE_EXPERTS_PER_GROUP = 8
MOE_EXPERTS = MOE_GROUPS * MOE_EXPERTS_PER_GROUP
MOE_TOP_K = 2
MOE_FF = D_MODEL // 4
MOE_BLOCK = 128

IN_SPLITS = (ATT_GROUPS * ATT_WIDTH,
             ATT_GROUPS * ATT_WIDTH,
             ATT_WIDTH,
             GDN_QK_WIDTH,
             GDN_QK_WIDTH,
             GDN_V_WIDTH,
             GDN_V_WIDTH,
             2 * GDN_V_HEADS,
             2 * GDN_V_HEADS,
             D_MODEL,
             D_MODEL)
IN_WIDTH = sum(IN_SPLITS)
IN_OFFSETS = tuple(int(v) for v in np.cumsum(IN_SPLITS[:-1]))

kernel_name = 'hybrid_dilated_attn_gdn_hmoe_encoder'


def rms_norm(x, gain):
    xf = x.astype(jnp.float32)
    return xf * lax.rsqrt(jnp.mean(xf * xf, axis=-1, keepdims=True) + NORM_EPS) * gain.astype(jnp.float32)


def l2_norm(x):
    return x * lax.rsqrt(jnp.sum(x * x, axis=-1, keepdims=True) + NORM_EPS)


def alibi_slopes():
    n = ATT_GROUPS * ATT_HEADS
    s = jnp.exp2(-ALIBI_MAX_BIAS * jnp.arange(1, n + 1, dtype=jnp.float32) / n)
    return s.reshape(ATT_GROUPS, ATT_HEADS)


def dilated_window_attention(q, k, v, slopes, window, dilation):
    b, s, h, c = q.shape
    half = window // (2 * dilation)
    blk = half
    sub_len = s // dilation
    n_blk = -(-sub_len // blk)
    pad = n_blk * blk - sub_len

    def to_blocks(a):
        cc = a.shape[-1]
        a = a.reshape(b, sub_len, dilation, h, cc).transpose(0, 2, 1, 3, 4)
        a = jnp.pad(a, ((0, 0), (0, 0), (0, pad), (0, 0), (0, 0)))
        return a.reshape(b, dilation, n_blk, blk, h, cc)

    def from_blocks(a):
        cc = a.shape[-1]
        a = a.reshape(b, dilation, n_blk * blk, h, cc)[:, :, :sub_len]
        return a.transpose(0, 2, 1, 3, 4).reshape(b, s, h, cc)

    def with_neighbours(a):
        a = jnp.pad(a, ((0, 0), (0, 0), (1, 1), (0, 0), (0, 0), (0, 0)))
        return jnp.concatenate([a[:, :, :-2], a[:, :, 1:-1], a[:, :, 2:]], axis=3)

    qb = to_blocks(q)
    kb = with_neighbours(to_blocks(k))
    vb = with_neighbours(to_blocks(v))
    q_pos = jnp.arange(n_blk * blk).reshape(n_blk, blk)
    k_pos = (jnp.arange(n_blk)[:, None] - 1) * blk + jnp.arange(3 * blk)[None, :]
    rel = k_pos[:, None, :] - q_pos[:, :, None]
    valid = (jnp.abs(rel) <= half) & (k_pos[:, None, :] >= 0) & (k_pos[:, None, :] < sub_len)
    dist = (jnp.abs(rel) * dilation).astype(jnp.float32)
    scores = jnp.einsum('brnqhc,brnkhc->brhnqk', qb, kb) * (c ** -0.5)
    scores = scores - slopes[:, None, None, None] * dist
    scores = jnp.where(valid, scores, -1e30)
    m = jnp.max(scores, axis=-1, keepdims=True)
    p = jnp.exp(scores - m)
    l = jnp.sum(p, axis=-1, keepdims=True)
    out = jnp.einsum('brhnqk,brnkhc->brnqhc', p / l, vb)
    lse = jnp.transpose((m + jnp.log(l))[..., 0], (0, 1, 3, 4, 2))
    return from_blocks(out), from_blocks(lse[..., None])[..., 0]


def dilated_attention_mixer(q_all, k_all, v, q_gain, k_gain):
    b, s = v.shape[:2]
    q_all = rms_norm(q_all, q_gain[:, None, :])
    k_all = rms_norm(k_all, k_gain[:, None, :])
    v = v.astype(jnp.float32)
    slopes = alibi_slopes()
    outs, lses = [], []
    for g, (window, dilation) in enumerate(ATT_PATTERNS):
        o, lse = dilated_window_attention(q_all[:, :, g], k_all[:, :, g], v, slopes[g], window, dilation)
        outs.append(o)
        lses.append(lse)
    weights = jax.nn.softmax(jnp.stack(lses, axis=0), axis=0)
    out = jnp.sum(weights[..., None] * jnp.stack(outs, axis=0), axis=0)
    return out.reshape(b, s, ATT_WIDTH)


def centred_depthwise_conv(x, w):
    width, ch = w.shape
    return lax.conv_general_dilated(
        x, w[:, None, :].astype(x.dtype), window_strides=(1,),
        padding=[((width - 1) // 2, width // 2)],
        dimension_numbers=('NWC', 'WIO', 'NWC'), feature_group_count=ch)


def chunk_gated_delta_rule(q, k, v, g, beta):
    b, s, h, dk = k.shape
    dv = v.shape[-1]
    c = GDN_CHUNK
    n = s // c

    def chunked(t):
        t = t.reshape((b, n, c, h) + t.shape[3:])
        return jnp.moveaxis(t, 3, 1)

    q = chunked(q * (dk ** -0.5))
    k = chunked(k)
    v = chunked(v)
    g = jnp.cumsum(chunked(g), axis=-1)
    beta = chunked(beta)
    tri = jnp.tril(jnp.ones((c, c), dtype=bool))
    strict = jnp.tril(jnp.ones((c, c), dtype=bool), -1)
    diff = g[..., :, None] - g[..., None, :]
    decay = jnp.where(tri, jnp.exp(jnp.where(tri, diff, 0.0)), 0.0)
    k_beta = k * beta[..., None]
    lower = jnp.where(strict, jnp.einsum('bhnid,bhnjd->bhnij', k_beta, k) * decay, 0.0)
    eye = jnp.eye(c, dtype=jnp.float32)
    rhs = jnp.concatenate([v * beta[..., None], k_beta * jnp.exp(g)[..., None]], axis=-1)
    sol = lax.linalg.triangular_solve(jnp.broadcast_to(lower + eye, lower.shape), rhs,
                                      left_side=True, lower=True)
    u, w = sol[..., :dv], sol[..., dv:]
    intra = jnp.where(tri, jnp.einsum('bhnid,bhnjd->bhnij', q, k) * decay, 0.0)
    q_dec = q * jnp.exp(g)[..., None]
    k_tail = k * jnp.exp(g[..., -1:] - g)[..., None]
    chunk_decay = jnp.exp(g[..., -1])

    def step(state, inp):
        u_n, w_n, q_n, k_n, a_n, d_n = inp
        v_new = u_n - jnp.einsum('bhik,bhkv->bhiv', w_n, state)
        o_n = jnp.einsum('bhik,bhkv->bhiv', q_n, state) + jnp.einsum('bhij,bhjv->bhiv', a_n, v_new)
        state = state * d_n[..., None, None] + jnp.einsum('bhik,bhiv->bhkv', k_n, v_new)
        return state, o_n

    xs = tuple(jnp.moveaxis(t, 2, 0) for t in (u, w, q_dec, k_tail, intra, chunk_decay))
    state0 = jnp.zeros((b, h, dk, dv), jnp.float32)
    _, o = lax.scan(step, state0, xs)
    return jnp.transpose(o, (1, 0, 3, 2, 4)).reshape(b, s, h, dv)


def gated_deltanet_mixer(q, k, v, z, a, beta_logit, conv_w, a_log, dt_bias, norm_gain):
    b, s = q.shape[:2]
    f32 = jnp.float32
    qkv = jnp.concatenate([q, k, v], axis=-1).astype(f32)
    qkv = jax.nn.silu(centred_depthwise_conv(qkv, conv_w.astype(f32)))
    q, k, v = jnp.split(qkv, (GDN_QK_WIDTH, 2 * GDN_QK_WIDTH), axis=-1)
    rep = GDN_V_HEADS // GDN_QK_HEADS
    q = jnp.repeat(l2_norm(q.reshape(b, s, GDN_QK_HEADS, GDN_HEAD_DIM)), rep, axis=2)
    k = jnp.repeat(l2_norm(k.reshape(b, s, GDN_QK_HEADS, GDN_HEAD_DIM)), rep, axis=2)
    v = v.reshape(b, s, GDN_V_HEADS, GDN_HEAD_DIM)
    beta = jax.nn.sigmoid(beta_logit.astype(f32))
    g = -jnp.exp(a_log.astype(f32)) * jax.nn.softplus(a.astype(f32) + dt_bias.astype(f32))
    o_fwd = chunk_gated_delta_rule(q, k, v, g[:, :, 0], beta[:, :, 0])
    flip = lambda t: jnp.flip(t, axis=1)
    o_bwd = flip(chunk_gated_delta_rule(flip(q), flip(k), flip(v), flip(g[:, :, 1]), flip(beta[:, :, 1])))
    o = rms_norm(o_fwd + o_bwd, norm_gain) * jax.nn.silu(z.reshape(b, s, GDN_V_HEADS, GDN_HEAD_DIM).astype(f32))
    return o.reshape(b, s, GDN_V_WIDTH)


def hierarchical_moe(h, w_group_router, b_group_router, w_expert_router, b_expert_router, w_gate, w_up, w_down):
    b, s, d = h.shape
    t = b * s
    hf = h.reshape(t, d)
    group_prob = jax.nn.softmax((hf @ w_group_router).astype(jnp.float32) + b_group_router.astype(jnp.float32), axis=-1)
    group_w, group_id = lax.top_k(group_prob, 1)
    expert_logits = ((hf @ w_expert_router).astype(jnp.float32) + b_expert_router.astype(jnp.float32))
    expert_logits = expert_logits.reshape(t, MOE_GROUPS, MOE_EXPERTS_PER_GROUP)
    in_group = jnp.take_along_axis(expert_logits, group_id[:, :, None], axis=1)[:, 0]
    local_w, local_id = lax.top_k(jax.nn.softmax(in_group, axis=-1), MOE_TOP_K)
    local_w = local_w / jnp.sum(local_w, axis=-1, keepdims=True)
    expert_id = group_id * MOE_EXPERTS_PER_GROUP + local_id
    weight = group_w * local_w

    n_assign = t * MOE_TOP_K
    n_blocks = -(-n_assign // MOE_BLOCK) + MOE_EXPERTS
    n_slots = n_blocks * MOE_BLOCK
    flat_e = expert_id.reshape(n_assign)
    flat_w = weight.reshape(n_assign)
    flat_t = jnp.repeat(jnp.arange(t, dtype=jnp.int32), MOE_TOP_K)
    order = jnp.argsort(flat_e)
    se, st, sw = flat_e[order], flat_t[order], flat_w[order]
    counts = jax.ops.segment_sum(jnp.ones((n_assign,), jnp.int32), flat_e, num_segments=MOE_EXPERTS)
    starts = jnp.cumsum(counts) - counts
    padded = (counts + MOE_BLOCK - 1) // MOE_BLOCK * MOE_BLOCK
    padded_ends = jnp.cumsum(padded)
    padded_starts = padded_ends - padded
    dest = padded_starts[se] + jnp.arange(n_assign, dtype=jnp.int32) - starts[se]
    slot_tok = jnp.full((n_slots,), t, jnp.int32).at[dest].set(st)
    slot_w = jnp.zeros((n_slots,), jnp.float32).at[dest].set(sw)
    block_expert = jnp.clip(jnp.searchsorted(padded_ends, jnp.arange(n_blocks) * MOE_BLOCK, side='right'),
                            0, MOE_EXPERTS - 1)
    xs = jnp.concatenate([hf, jnp.zeros((1, d), hf.dtype)], axis=0)[slot_tok].reshape(n_blocks, MOE_BLOCK, d)

    def expert_block(args):
        xb, e = args
        return (jax.nn.silu(xb @ w_gate[e]) * (xb @ w_up[e])) @ w_down[e]

    y = lax.map(expert_block, (xs, block_expert)).reshape(n_slots, d)
    out = jax.ops.segment_sum(y * slot_w[:, None], slot_tok, num_segments=t + 1)[:t]
    return out.reshape(b, s, d)


def setup_inputs(seed: int = 0) -> dict:
    key = jax.random.key(seed)
    ks = jax.random.split(key, 20)
    f32 = jnp.float32
    L = DEPTH

    def normal(k, shape, fan_in):
        return jax.random.normal(k, shape, f32) * (fan_in ** -0.5)

    def gain(k, shape):
        return 1.0 + 0.02 * jax.random.normal(k, shape, f32)

    dt = jnp.exp(jax.random.uniform(ks[7], (L, 2, GDN_V_HEADS), f32, math.log(1e-3), math.log(1e-1)))
    return {
        'x': jax.random.normal(ks[0], (BATCH, SEQ, D_MODEL), f32),
        'norm1_gain': gain(ks[1], (L, D_MODEL)),
        'w_in': normal(ks[2], (L, D_MODEL, IN_WIDTH), D_MODEL),
        'q_norm_gain': gain(ks[3], (L, ATT_GROUPS, ATT_HEAD_DIM)),
        'k_norm_gain': gain(ks[4], (L, ATT_GROUPS, ATT_HEAD_DIM)),
        'gdn_conv_w': normal(ks[5], (L, GDN_CONV_WIDTH, GDN_CONV_CH), GDN_CONV_WIDTH),
        'gdn_a_log': jnp.log(jax.random.uniform(ks[6], (L, 2, GDN_V_HEADS), f32, 1.0, 16.0)),
        'gdn_dt_bias': dt + jnp.log(-jnp.expm1(-dt)),
        'gdn_norm_gain': gain(ks[8], (L, GDN_HEAD_DIM)),
        'w_branch_att': normal(ks[9], (L, ATT_WIDTH, D_MODEL), ATT_WIDTH),
        'w_branch_gdn': normal(ks[10], (L, GDN_V_WIDTH, D_MODEL), GDN_V_WIDTH),
        'w_out': normal(ks[11], (L, D_MODEL, D_MODEL), D_MODEL),
        'norm2_gain': gain(ks[12], (L, D_MODEL)),
        'w_group_router': normal(ks[13], (L, D_MODEL, MOE_GROUPS), D_MODEL),
        'b_group_router': 0.01 * jax.random.normal(ks[14], (L, MOE_GROUPS), f32),
        'w_expert_router': normal(ks[15], (L, D_MODEL, MOE_EXPERTS), D_MODEL),
        'b_expert_router': 0.01 * jax.random.normal(ks[16], (L, MOE_EXPERTS), f32),
        'w_gate': normal(ks[17], (L, MOE_EXPERTS, D_MODEL, MOE_FF), D_MODEL),
        'w_up': normal(ks[18], (L, MOE_EXPERTS, D_MODEL, MOE_FF), D_MODEL),
        'w_down': normal(ks[19], (L, MOE_EXPERTS, MOE_FF, D_MODEL), MOE_FF),
    }


def reference(x, norm1_gain, w_in, q_norm_gain, k_norm_gain, gdn_conv_w, gdn_a_log, gdn_dt_bias,
              gdn_norm_gain, w_branch_att, w_branch_gdn, w_out, norm2_gain, w_group_router,
              b_group_router, w_expert_router, b_expert_router, w_gate, w_up, w_down):
    dt = x.dtype
    b, s = x.shape[:2]
    for i in range(DEPTH):
        h = rms_norm(x, norm1_gain[i]).astype(dt)
        proj = h @ w_in[i]
        (q_att, k_att, v_att, q_gdn, k_gdn, v_gdn, z_gdn, a_gdn, b_gdn,
         gate_att, gate_gdn) = jnp.split(proj, IN_OFFSETS, axis=-1)
        y_att = dilated_attention_mixer(
            q_att.reshape(b, s, ATT_GROUPS, ATT_HEADS, ATT_HEAD_DIM),
            k_att.reshape(b, s, ATT_GROUPS, ATT_HEADS, ATT_HEAD_DIM),
            v_att.reshape(b, s, ATT_HEADS, ATT_HEAD_DIM),
            q_norm_gain[i], k_norm_gain[i]).astype(dt)
        y_gdn = gated_deltanet_mixer(
            q_gdn, k_gdn, v_gdn, z_gdn,
            a_gdn.reshape(b, s, 2, GDN_V_HEADS), b_gdn.reshape(b, s, 2, GDN_V_HEADS),
            gdn_conv_w[i], gdn_a_log[i], gdn_dt_bias[i], gdn_norm_gain[i]).astype(dt)
        merged = (jax.nn.sigmoid(gate_att) * (y_att @ w_branch_att[i])
                  + jax.nn.sigmoid(gate_gdn) * (y_gdn @ w_branch_gdn[i]))
        x = x + merged @ w_out[i]
        h2 = rms_norm(x, norm2_gain[i]).astype(dt)
        x = x + hierarchical_moe(h2, w_group_router[i], b_group_router[i], w_expert_router[i],
                                 b_expert_router[i], w_gate[i], w_up[i], w_down[i]).astype(dt)
    return x
```

```python
import functools

import jax
import jax.numpy as jnp
from jax import lax
from jax.experimental import pallas as pl
from jax.experimental.pallas import tpu as pltpu

F32 = jnp.float32
BF16 = jnp.bfloat16
HIGHEST = lax.Precision.HIGHEST

NORM_EPS = 1e-6
D_MODEL = 2048
HEAD_DIM = 128
ATT_HEADS = 16
ATT_PATTERNS = ((128, 1), (512, 4), (2048, 16))
ATT_GROUPS = len(ATT_PATTERNS)
ALIBI_MAX_BIAS = 8.0
GDN_QK_HEADS = 16
GDN_V_HEADS = 32
GDN_CONV_WIDTH = 5
GDN_CHUNK = 64
MOE_GROUPS = 8
MOE_EXPERTS_PER_GROUP = 8
MOE_EXPERTS = 64
MOE_FF = 512
MOE_BLOCK = 128

ATT_W = ATT_HEADS * HEAD_DIM
QK_COLS = 2 * ATT_GROUPS * ATT_W
HEADS_COLS = ATT_W + 2 * GDN_QK_HEADS * HEAD_DIM + 2 * GDN_V_HEADS * HEAD_DIM
AB_OFF = QK_COLS + HEADS_COLS
AB_COLS = 4 * GDN_V_HEADS
GATE_OFF = AB_OFF + AB_COLS
GATE_COLS = 2 * D_MODEL

VMEM_LIMIT = 56 * 1024 * 1024
LANES = 128


def _cparams(sem, vmem=VMEM_LIMIT):
    return pltpu.CompilerParams(dimension_semantics=sem, vmem_limit_bytes=vmem)


def _sigmoid(x):
    return 1.0 / (1.0 + jnp.exp(-x))


def _softplus(x):
    return jnp.maximum(x, 0.0) + jnp.log(1.0 + jnp.exp(-jnp.abs(x)))


def _rmsnorm_kernel(x_ref, g_ref, o_ref):
    x = x_ref[...]
    ms = jnp.mean(x * x, axis=-1, keepdims=True)
    o_ref[...] = (x * lax.rsqrt(ms + NORM_EPS) * g_ref[...]).astype(o_ref.dtype)


def _rmsnorm(x, gain, tm=512):
    s, d = x.shape
    return pl.pallas_call(
        _rmsnorm_kernel,
        grid=(s // tm,),
        in_specs=[pl.BlockSpec((tm, d), lambda i: (i, 0)), pl.BlockSpec((1, d), lambda i: (0, 0))],
        out_specs=pl.BlockSpec((tm, d), lambda i: (i, 0)),
        out_shape=jax.ShapeDtypeStruct((s, d), BF16),
        compiler_params=_cparams(("parallel",)),
    )(x, gain.reshape(1, d))


def _inproj_kernel(*refs, mode, tn):
    if mode == "qknorm":
        x_ref, w_ref, g_ref, o_ref, wb_ref = refs
    else:
        x_ref, w_ref, o_ref, wb_ref = refs

    @pl.when(pl.program_id(1) == 0)
    def _():
        wb_ref[...] = w_ref[...].astype(BF16)

    acc = jnp.dot(x_ref[...], wb_ref[...], preferred_element_type=F32)
    if mode in ("qknorm", "heads"):
        for c in range(tn // HEAD_DIM):
            a = acc[:, c * HEAD_DIM:(c + 1) * HEAD_DIM]
            if mode == "qknorm":
                ms = jnp.mean(a * a, axis=-1, keepdims=True)
                a = a * lax.rsqrt(ms + NORM_EPS) * g_ref[:, c * HEAD_DIM:(c + 1) * HEAD_DIM]
            o_ref[c] = a.astype(o_ref.dtype)
    elif mode == "f32":
        o_ref[...] = acc
    else:
        o_ref[...] = _sigmoid(acc).astype(o_ref.dtype)


def _inproj(h, w, col_off, ncols, mode, gain=None, tm=1024, tn=1024):
    s, d = h.shape
    assert col_off % tn == 0 and ncols % tn == 0 and s % tm == 0
    joff = col_off // tn
    in_specs = [pl.BlockSpec((tm, d), lambda j, i: (i, 0)),
                pl.BlockSpec((d, tn), lambda j, i: (0, joff + j))]
    args = [h, w]
    if mode == "qknorm":
        in_specs.append(pl.BlockSpec((1, tn), lambda j, i: (0, j)))
        args.append(gain)
    if mode in ("qknorm", "heads"):
        hpt = tn // HEAD_DIM
        out_spec = pl.BlockSpec((hpt, tm, HEAD_DIM), lambda j, i: (j, i, 0))
        out_shape = jax.ShapeDtypeStruct((ncols // HEAD_DIM, s, HEAD_DIM), BF16)
    else:
        out_spec = pl.BlockSpec((tm, tn), lambda j, i: (i, j))
        out_shape = jax.ShapeDtypeStruct((s, ncols), F32 if mode == "f32" else BF16)
    return pl.pallas_call(
        functools.partial(_inproj_kernel, mode=mode, tn=tn),
        grid=(ncols // tn, s // tm),
        in_specs=in_specs,
        out_specs=out_spec,
        out_shape=out_shape,
        scratch_shapes=[pltpu.VMEM((d, tn), BF16)],
        compiler_params=_cparams(("parallel", "arbitrary")),
    )(*args)


def _attn_kernel(slope_ref, q_ref, k_ref, v_ref, o_ref, lse_ref, *, length, tq, half):
    slope = slope_ref[pl.program_id(0)]
    win = tq + 2 * half
    base = (lax.broadcasted_iota(jnp.int32, (tq, win), 0)
            - lax.broadcasted_iota(jnp.int32, (tq, win), 1))
    eye = (lax.broadcasted_iota(jnp.int32, (tq, tq), 0)
           == lax.broadcasted_iota(jnp.int32, (tq, tq), 1))

    def body(i, carry):
        q0 = pl.multiple_of(i * tq, tq)
        start = pl.multiple_of(jnp.clip(q0 - half, 0, length - win), half)
        q = q_ref[pl.ds(q0, tq), :]
        k = k_ref[pl.ds(start, win), :]
        v = v_ref[pl.ds(start, win), :]
        s = lax.dot_general(q, k, (((1,), (1,)), ((), ())), preferred_element_type=F32)
        dist = jnp.abs(base + (q0 - start))
        s = s - slope * dist.astype(F32)
        s = jnp.where(dist <= half, s, -1e30)
        m = jnp.max(s, axis=-1, keepdims=True)
        p = jnp.exp(s - m)
        l = jnp.sum(p, axis=-1, keepdims=True)
        pv = jnp.dot(p.astype(BF16), v, preferred_element_type=F32)
        o_ref[pl.ds(q0, tq), :] = (pv / l).astype(o_ref.dtype)
        lse = m + jnp.log(l)
        lse_ref[pl.ds(i, 1), :] = jnp.sum(jnp.where(eye, lse, 0.0), axis=0, keepdims=True)
        return carry

    lax.fori_loop(0, length // tq, body, 0)


def _attn_group(qk_heads, heads, slopes, g, window, dil, tq=128):
    n_qk, s, c = qk_heads.shape
    length = s // dil
    half = window // (2 * dil)
    assert length >= tq + 2 * half and length % tq == 0
    qv = qk_heads.reshape(n_qk, length, dil * c)
    vv = heads.reshape(heads.shape[0], length, dil * c)
    n_t = length // tq
    grid_spec = pltpu.PrefetchScalarGridSpec(
        num_scalar_prefetch=1,
        grid=(ATT_HEADS, dil),
        in_specs=[
            pl.BlockSpec((None, length, c), lambda h, r, sl: (g * ATT_HEADS + h, 0, r)),
            pl.BlockSpec((None, length, c), lambda h, r, sl: ((ATT_GROUPS + g) * ATT_HEADS + h, 0, r)),
            pl.BlockSpec((None, length, c), lambda h, r, sl: (h, 0, r)),
        ],
        out_specs=[
            pl.BlockSpec((None, length, c), lambda h, r, sl: (h, 0, r)),
            pl.BlockSpec((None, None, n_t, tq), lambda h, r, sl: (h, r, 0, 0)),
        ],
    )
    o, lse = pl.pallas_call(
        functools.partial(_attn_kernel, length=length, tq=tq, half=half),
        grid_spec=grid_spec,
        out_shape=[jax.ShapeDtypeStruct((ATT_HEADS, length, dil * c), BF16),
                   jax.ShapeDtypeStruct((ATT_HEADS, dil, n_t, tq), F32)],
        compiler_params=_cparams(("parallel", "parallel")),
    )(slopes * float(dil), qv, qv, vv)
    o = o.reshape(ATT_HEADS, s, c)
    lse = lse.reshape(ATT_HEADS, dil, length).transpose(2, 1, 0).reshape(s, ATT_HEADS)
    return o, lse


def _attn_combine_kernel(o1_ref, o2_ref, o3_ref, l1_ref, l2_ref, l3_ref, y_ref):
    l1, l2, l3 = l1_ref[...], l2_ref[...], l3_ref[...]
    m = jnp.maximum(jnp.maximum(l1, l2), l3)
    e1, e2, e3 = jnp.exp(l1 - m), jnp.exp(l2 - m), jnp.exp(l3 - m)
    den = e1 + e2 + e3
    w1, w2, w3 = e1 / den, e2 / den, e3 / den
    for h in range(ATT_HEADS):
        y = (w1[:, h:h + 1] * o1_ref[h].astype(F32) + w2[:, h:h + 1] * o2_ref[h].astype(F32)
             + w3[:, h:h + 1] * o3_ref[h].astype(F32))
        y_ref[:, h * HEAD_DIM:(h + 1) * HEAD_DIM] = y.astype(y_ref.dtype)


def _attn_combine(outs, lses, ts=512):
    _, s, c = outs[0].shape
    ospec = pl.BlockSpec((ATT_HEADS, ts, c), lambda i: (0, i, 0))
    lspec = pl.BlockSpec((ts, ATT_HEADS), lambda i: (i, 0))
    return pl.pallas_call(
        _attn_combine_kernel,
        grid=(s // ts,),
        in_specs=[ospec] * 3 + [lspec] * 3,
        out_specs=pl.BlockSpec((ts, ATT_W), lambda i: (i, 0)),
        out_shape=jax.ShapeDtypeStruct((s, ATT_W), BF16),
        compiler_params=_cparams(("parallel",)),
    )(*outs, *lses)


def _gdn_conv_kernel(x_ref, p_ref, n_ref, w_ref, o_ref, *, tr, halo):
    c = pl.program_id(0)
    i = pl.program_id(1)
    x = x_ref[...].astype(F32)
    prev = jnp.where(i > 0, p_ref[...].astype(F32), 0.0)
    nxt = jnp.where(i < pl.num_programs(1) - 1, n_ref[...].astype(F32), 0.0)
    xc = jnp.concatenate([prev, x, nxt], axis=0)
    n = tr + 2 * halo
    w = w_ref[...]
    y = jnp.zeros((tr, HEAD_DIM), F32)
    for j in range(GDN_CONV_WIDTH):
        shift = (GDN_CONV_WIDTH - 1) // 2 - j
        xs = xc if shift == 0 else pltpu.roll(xc, shift % n, 0)
        y = y + xs[halo:halo + tr] * w[j:j + 1]
    y = y * _sigmoid(y)
    nrm = lax.rsqrt(jnp.sum(y * y, axis=-1, keepdims=True) + NORM_EPS)
    scale = jnp.where(c < GDN_QK_HEADS, nrm * (HEAD_DIM ** -0.5),
                      jnp.where(c < 2 * GDN_QK_HEADS, nrm, 1.0))
    o_ref[...] = (y * scale).astype(o_ref.dtype)


def _gdn_conv(heads, conv_w, first_head, tr=512, halo=16):
    _, s, c = heads.shape
    n_heads = 2 * GDN_QK_HEADS + GDN_V_HEADS
    hb = tr // halo
    n_hb = s // halo
    return pl.pallas_call(
        functools.partial(_gdn_conv_kernel, tr=tr, halo=halo),
        grid=(n_heads, s // tr),
        in_specs=[
            pl.BlockSpec((None, tr, c), lambda h, i: (first_head + h, i, 0)),
            pl.BlockSpec((None, halo, c), lambda h, i: (first_head + h, jnp.maximum(i * hb - 1, 0), 0)),
            pl.BlockSpec((None, halo, c), lambda h, i: (first_head + h, jnp.minimum((i + 1) * hb, n_hb - 1), 0)),
            pl.BlockSpec((GDN_CONV_WIDTH, c), lambda h, i: (0, h)),
        ],
        out_specs=pl.BlockSpec((None, tr, c), lambda h, i: (h, i, 0)),
        out_shape=jax.ShapeDtypeStruct((n_heads, s, c), BF16),
        compiler_params=_cparams(("parallel", "parallel")),
    )(heads, heads, heads, conv_w)


def _gdn_kernel(q_ref, k_ref, v_ref, z_ref, gb_ref, prm_ref, ng_ref, o_ref, acc_ref, *, seq, chunk):
    n_chunks = seq // chunk
    prm = prm_ref[...]
    ii = lax.broadcasted_iota(jnp.int32, (chunk, chunk), 0)
    jj = lax.broadcasted_iota(jnp.int32, (chunk, chunk), 1)
    eye = ii == jj
    row = lax.broadcasted_iota(jnp.int32, (chunk, LANES), 0)
    acc_ref[...] = jnp.zeros_like(acc_ref)

    def chain(n, state, d):
        r0 = pl.multiple_of(n * chunk, chunk)
        qb = q_ref[pl.ds(r0, chunk), :]
        kb = k_ref[pl.ds(r0, chunk), :]
        qc, kc = qb.astype(F32), kb.astype(F32)
        vc = v_ref[pl.ds(r0, chunk), :].astype(F32)
        gbt = gb_ref[pl.ds(r0, chunk), :]
        g = -jnp.exp(prm[:, d:d + 1]) * _softplus(gbt[:, d:d + 1] + prm[:, 2 + d:3 + d])
        beta = jnp.broadcast_to(_sigmoid(gbt[:, 2 + d:3 + d]), (chunk, LANES))
        gc = jnp.broadcast_to(g, (chunk, LANES))
        sh = 1
        while sh < chunk:
            if d == 0:
                gc = gc + jnp.where(row >= sh, pltpu.roll(gc, sh, 0), 0.0)
            else:
                gc = gc + jnp.where(row < chunk - sh, pltpu.roll(gc, chunk - sh, 0), 0.0)
            sh *= 2
        tot = gc[chunk - 1:chunk, :] if d == 0 else gc[0:1, :]
        gcc = gc[:, :chunk]
        gc_row = jnp.sum(jnp.where(eye, gcc, 0.0), axis=0, keepdims=True)
        incl = (ii >= jj) if d == 0 else (ii <= jj)
        strict = (ii > jj) if d == 0 else (ii < jj)
        decay = jnp.exp(jnp.where(incl, gcc - gc_row, 0.0))
        kk = lax.dot_general(kb, kb, (((1,), (1,)), ((), ())), preferred_element_type=F32)
        qk = lax.dot_general(qb, kb, (((1,), (1,)), ((), ())), preferred_element_type=F32)
        low = jnp.where(strict, beta[:, :chunk] * kk * decay, 0.0)
        intra = jnp.where(incl, qk * decay, 0.0)
        eg = jnp.exp(gc)
        x = jnp.concatenate([vc * beta, kc * (beta * eg)], axis=1)
        p = -low
        steps = chunk.bit_length() - 1
        for t in range(steps):
            x = x + jnp.dot(p.astype(BF16), x.astype(BF16), preferred_element_type=F32)
            if t < steps - 1:
                p = jnp.dot(p.astype(BF16), p.astype(BF16), preferred_element_type=F32)
        u, w = x[:, :HEAD_DIM], x[:, HEAD_DIM:]
        q_dec = qc * eg
        k_tail = kc * jnp.exp(tot - gc)
        sb = state.astype(BF16)
        wq = jnp.concatenate([w, q_dec], axis=0).astype(BF16)
        ws = jnp.dot(wq, sb, preferred_element_type=F32)
        v_new = u - ws[:chunk]
        vb = v_new.astype(BF16)
        o = ws[chunk:] + jnp.dot(intra.astype(BF16), vb, preferred_element_type=F32)
        acc_ref[pl.ds(r0, chunk), :] += o
        upd = lax.dot_general(k_tail.astype(BF16), vb, (((0,), (0,)), ((), ())),
                              preferred_element_type=F32)
        return state * jnp.exp(tot[:, :HEAD_DIM]) + upd

    def body(n, carry):
        sf, sb = carry
        return chain(n, sf, 0), chain(n_chunks - 1 - n, sb, 1)

    zero = jnp.zeros((HEAD_DIM, HEAD_DIM), F32)
    lax.fori_loop(0, n_chunks, body, (zero, zero))

    tile = 256

    def fin(i, carry):
        r0 = pl.multiple_of(i * tile, tile)
        o = acc_ref[pl.ds(r0, tile), :]
        z = z_ref[pl.ds(r0, tile), :].astype(F32)
        ms = jnp.mean(o * o, axis=-1, keepdims=True)
        y = o * lax.rsqrt(ms + NORM_EPS) * ng_ref[...] * (z * _sigmoid(z))
        o_ref[pl.ds(r0, tile), :] = y.astype(o_ref.dtype)
        return carry

    lax.fori_loop(0, seq // tile, fin, 0)


def _gdn(gq, heads, z_first, gb, prm, norm_gain):
    _, s, c = gq.shape
    full = lambda f: pl.BlockSpec((None, s, c), f)
    return pl.pallas_call(
        functools.partial(_gdn_kernel, seq=s, chunk=GDN_CHUNK),
        grid=(GDN_V_HEADS,),
        in_specs=[
            full(lambda h: (h // 2, 0, 0)),
            full(lambda h: (GDN_QK_HEADS + h // 2, 0, 0)),
            full(lambda h: (2 * GDN_QK_HEADS + h, 0, 0)),
            full(lambda h: (z_first + h, 0, 0)),
            pl.BlockSpec((None, s, 4), lambda h: (h, 0, 0)),
            pl.BlockSpec((None, 1, 4), lambda h: (h, 0, 0)),
            pl.BlockSpec((1, c), lambda h: (0, 0)),
        ],
        out_specs=pl.BlockSpec((s, c), lambda h: (0, h)),
        out_shape=jax.ShapeDtypeStruct((s, GDN_V_HEADS * c), BF16),
        scratch_shapes=[pltpu.VMEM((s, c), F32)],
        compiler_params=_cparams(("parallel",)),
    )(gq, gq, gq, heads, gb, prm, norm_gain)


def _branch_kernel(ya_ref, yg_ref, wa_ref, wg_ref, ga_ref, gg_ref, o_ref, wab_ref, wgb_ref):
    @pl.when(pl.program_id(1) == 0)
    def _():
        wab_ref[...] = wa_ref[...].astype(BF16)
        wgb_ref[...] = wg_ref[...].astype(BF16)

    a = jnp.dot(ya_ref[...], wab_ref[...], preferred_element_type=F32)
    g = jnp.dot(yg_ref[...], wgb_ref[...], preferred_element_type=F32)
    o_ref[...] = (ga_ref[...].astype(F32) * a + gg_ref[...].astype(F32) * g).astype(o_ref.dtype)


def _branch(y_att, y_gdn, w_a, w_g, gates, tm=1024, tn=256):
    s, da = y_att.shape
    dg = y_gdn.shape[1]
    n = w_a.shape[1]
    goff = n // tn
    return pl.pallas_call(
        _branch_kernel,
        grid=(n // tn, s // tm),
        in_specs=[
            pl.BlockSpec((tm, da), lambda j, i: (i, 0)),
            pl.BlockSpec((tm, dg), lambda j, i: (i, 0)),
            pl.BlockSpec((da, tn), lambda j, i: (0, j)),
            pl.BlockSpec((dg, tn), lambda j, i: (0, j)),
            pl.BlockSpec((tm, tn), lambda j, i: (i, j)),
            pl.BlockSpec((tm, tn), lambda j, i: (i, goff + j)),
        ],
        out_specs=pl.BlockSpec((tm, tn), lambda j, i: (i, j)),
        out_shape=jax.ShapeDtypeStruct((s, n), BF16),
        scratch_shapes=[pltpu.VMEM((da, tn), BF16), pltpu.VMEM((dg, tn), BF16)],
        compiler_params=_cparams(("parallel", "arbitrary")),
    )(y_att, y_gdn, w_a, w_g, gates, gates)


def _outproj_kernel(m_ref, w_ref, x_ref, o_ref, wb_ref):
    @pl.when(pl.program_id(1) == 0)
    def _():
        wb_ref[...] = w_ref[...].astype(BF16)

    o_ref[...] = x_ref[...] + jnp.dot(m_ref[...], wb_ref[...], preferred_element_type=F32)


def _outproj(merged, w, x, tm=1024, tn=512):
    s, d = merged.shape
    n = w.shape[1]
    return pl.pallas_call(
        _outproj_kernel,
        grid=(n // tn, s // tm),
        in_specs=[
            pl.BlockSpec((tm, d), lambda j, i: (i, 0)),
            pl.BlockSpec((d, tn), lambda j, i: (0, j)),
            pl.BlockSpec((tm, tn), lambda j, i: (i, j)),
        ],
        out_specs=pl.BlockSpec((tm, tn), lambda j, i: (i, j)),
        out_shape=jax.ShapeDtypeStruct((s, n), F32),
        scratch_shapes=[pltpu.VMEM((d, tn), BF16)],
        compiler_params=_cparams(("parallel", "arbitrary")),
    )(merged, w, x)


def _router_kernel(x_ref, g_ref, wr_ref, br_ref, h_ref, route_ref, wa_ref, wb_ref):
    x = x_ref[...]
    ms = jnp.mean(x * x, axis=-1, keepdims=True)
    h = x * lax.rsqrt(ms + NORM_EPS) * g_ref[...]
    h_ref[...] = h.astype(h_ref.dtype)
    logits = jnp.dot(h, wr_ref[...], precision=HIGHEST, preferred_element_type=F32) + br_ref[...]
    lane_i = lax.broadcasted_iota(jnp.int32, logits.shape, 1)
    lane = lane_i.astype(F32)
    neg = -1e30
    big = 1e6
    is_g = lane_i < MOE_GROUPS
    lg = jnp.where(is_g, logits, neg)
    mg = jnp.max(lg, axis=-1, keepdims=True)
    sg = jnp.sum(jnp.where(is_g, jnp.exp(lg - mg), 0.0), axis=-1, keepdims=True)
    group_w = 1.0 / sg
    gid = jnp.min(jnp.where(is_g & (lg == mg), lane, big), axis=-1, keepdims=True)
    e_lane = lane_i - MOE_GROUPS
    in_grp = (e_lane >= 0) & (e_lane < MOE_EXPERTS) & ((e_lane // MOE_EXPERTS_PER_GROUP).astype(F32) == gid)
    le = jnp.where(in_grp, logits, neg)
    m1 = jnp.max(le, axis=-1, keepdims=True)
    i1 = jnp.min(jnp.where(in_grp & (le == m1), lane, big), axis=-1, keepdims=True)
    rest = in_grp & (lane != i1)
    le2 = jnp.where(rest, logits, neg)
    m2 = jnp.max(le2, axis=-1, keepdims=True)
    i2 = jnp.min(jnp.where(rest & (le2 == m2), lane, big), axis=-1, keepdims=True)
    se = jnp.sum(jnp.where(in_grp, jnp.exp(le - m1), 0.0), axis=-1, keepdims=True)
    p1 = 1.0 / se
    p2 = jnp.exp(m2 - m1) / se
    den = p1 + p2
    w1 = group_w * (p1 / den)
    w2 = group_w * (p2 / den)
    e1 = i1 - MOE_GROUPS
    e2 = i2 - MOE_GROUPS
    route_ref[...] = jnp.where(lane_i == 0, e1, jnp.where(lane_i == 1, e2, 0.0))
    wa_ref[...] = jnp.broadcast_to(w1, wa_ref.shape)
    wb_ref[...] = jnp.broadcast_to(w2, wb_ref.shape)


def _router(x1, gain, w_router, b_router, tm=512):
    s, d = x1.shape
    row = pl.BlockSpec((tm, LANES), lambda i: (i, 0))
    return pl.pallas_call(
        _router_kernel,
        grid=(s // tm,),
        in_specs=[
            pl.BlockSpec((tm, d), lambda i: (i, 0)),
            pl.BlockSpec((1, d), lambda i: (0, 0)),
            pl.BlockSpec((d, LANES), lambda i: (0, 0)),
            pl.BlockSpec((1, LANES), lambda i: (0, 0)),
        ],
        out_specs=[pl.BlockSpec((tm, d), lambda i: (i, 0)), row, row, row],
        out_shape=[jax.ShapeDtypeStruct((s, d), BF16)] + [jax.ShapeDtypeStruct((s, LANES), F32)] * 3,
        compiler_params=_cparams(("parallel",)),
    )(x1, gain.reshape(1, d), w_router, b_router)


def _rank_kernel(route_ref, rank_ref, cnt_ref, run_ref, *, tm):
    @pl.when(pl.program_id(0) == 0)
    def _():
        run_ref[...] = jnp.zeros_like(run_ref)

    r = route_ref[...]
    lane_i = lax.broadcasted_iota(jnp.int32, r.shape, 1)
    lane = lane_i.astype(F32)
    oa = (lane == r[:, 0:1]).astype(F32)
    ob = (lane == r[:, 1:2]).astype(F32)
    both = oa + ob
    tri = (lax.broadcasted_iota(jnp.int32, (tm, tm), 0)
           > lax.broadcasted_iota(jnp.int32, (tm, tm), 1)).astype(BF16)
    before = run_ref[0:1, :] + jnp.dot(tri, both.astype(BF16), preferred_element_type=F32)
    ra = jnp.sum(oa * before, axis=-1, keepdims=True)
    rb = jnp.sum(ob * before, axis=-1, keepdims=True)
    rank_ref[...] = jnp.where(lane_i == 0, ra, jnp.where(lane_i == 1, rb, 0.0))
    run_ref[0:1, :] = run_ref[0:1, :] + jnp.sum(both, axis=0, keepdims=True)
    cnt_ref[...] = run_ref[...]


def _rank(route, tm=512):
    s = route.shape[0]
    return pl.pallas_call(
        functools.partial(_rank_kernel, tm=tm),
        grid=(s // tm,),
        in_specs=[pl.BlockSpec((tm, LANES), lambda i: (i, 0))],
        out_specs=[pl.BlockSpec((tm, LANES), lambda i: (i, 0)), pl.BlockSpec((8, LANES), lambda i: (0, 0))],
        out_shape=[jax.ShapeDtypeStruct((s, LANES), F32), jax.ShapeDtypeStruct((8, LANES), F32)],
        scratch_shapes=[pltpu.VMEM((8, LANES), F32)],
        compiler_params=_cparams(("arbitrary",)),
    )(route)


def _dispatch_kernel(da_ref, db_ref, h_ref, xs_in_ref, xs_ref, sem, *, seq, batch):
    del xs_in_ref

    def copy(t, slot):
        return pltpu.make_async_copy(h_ref.at[t], xs_ref.at[slot], sem)

    def do_batch(b, carry):
        def issue(t, c):
            tt = b * batch + t
            copy(tt, da_ref[tt]).start()
            copy(tt, db_ref[tt]).start()
            return c

        lax.fori_loop(0, batch, issue, 0)

        def drain(t, c):
            copy(0, 0).wait()
            copy(0, 0).wait()
            return c

        lax.fori_loop(0, batch, drain, 0)
        return carry

    lax.fori_loop(0, seq // batch, do_batch, 0)


def _dispatch(h2, dest_a, dest_b, n_slots, batch=256):
    s, d = h2.shape
    sub = d // LANES
    h3 = h2.reshape(s, sub, LANES)
    xs0 = jnp.zeros((n_slots, sub, LANES), h2.dtype)
    grid_spec = pltpu.PrefetchScalarGridSpec(
        num_scalar_prefetch=2,
        grid=(1,),
        in_specs=[pl.BlockSpec(memory_space=pl.ANY), pl.BlockSpec(memory_space=pl.ANY)],
        out_specs=pl.BlockSpec(memory_space=pl.ANY),
        scratch_shapes=[pltpu.SemaphoreType.DMA(())],
    )
    xs = pl.pallas_call(
        functools.partial(_dispatch_kernel, seq=s, batch=batch),
        grid_spec=grid_spec,
        out_shape=jax.ShapeDtypeStruct(xs0.shape, xs0.dtype),
        input_output_aliases={3: 0},
        compiler_params=_cparams(("arbitrary",)),
    )(dest_a, dest_b, h3, xs0)
    return xs.reshape(n_slots, d)


def _expert_kernel(be_ref, nu_ref, x_ref, wg_ref, wu_ref, wd_ref, y_ref, wgb_ref, wub_ref, wdb_ref):
    b = pl.program_id(0)

    @pl.when(b < nu_ref[0])
    def _():
        changed = jnp.logical_or(b == 0, be_ref[b] != be_ref[jnp.maximum(b - 1, 0)])

        @pl.when(changed)
        def _():
            wgb_ref[...] = wg_ref[...].astype(BF16)
            wub_ref[...] = wu_ref[...].astype(BF16)
            wdb_ref[...] = wd_ref[...].astype(BF16)

        x = x_ref[...]
        g = jnp.dot(x, wgb_ref[...], preferred_element_type=F32)
        u = jnp.dot(x, wub_ref[...], preferred_element_type=F32)
        mid = (g * _sigmoid(g) * u).astype(BF16)
        y_ref[...] = jnp.dot(mid, wdb_ref[...], preferred_element_type=F32)

    @pl.when(b >= nu_ref[0])
    def _():
        y_ref[...] = jnp.zeros_like(y_ref)


def _experts(xs, block_expert, n_used, w_gate, w_up, w_down, bm=MOE_BLOCK):
    n_slots, d = xs.shape
    ff = w_gate.shape[2]
    n_blocks = n_slots // bm
    blk = lambda b, be, nu: jnp.minimum(b, nu[0] - 1)
    grid_spec = pltpu.PrefetchScalarGridSpec(
        num_scalar_prefetch=2,
        grid=(n_blocks,),
        in_specs=[
            pl.BlockSpec((bm, d), lambda b, be, nu: (blk(b, be, nu), 0)),
            pl.BlockSpec((None, d, ff), lambda b, be, nu: (be[blk(b, be, nu)], 0, 0)),
            pl.BlockSpec((None, d, ff), lambda b, be, nu: (be[blk(b, be, nu)], 0, 0)),
            pl.BlockSpec((None, ff, d), lambda b, be, nu: (be[blk(b, be, nu)], 0, 0)),
        ],
        out_specs=pl.BlockSpec((bm, d), lambda b, be, nu: (b, 0)),
        scratch_shapes=[pltpu.VMEM((d, ff), BF16), pltpu.VMEM((d, ff), BF16), pltpu.VMEM((ff, d), BF16)],
    )
    return pl.pallas_call(
        _expert_kernel,
        grid_spec=grid_spec,
        out_shape=jax.ShapeDtypeStruct((n_slots, d), F32),
        compiler_params=_cparams(("arbitrary",)),
    )(block_expert, n_used, xs, w_gate, w_up, w_down)


def _combine_kernel(da_ref, db_ref, x_ref, wa_ref, wb_ref, y_ref, o_ref, bufa, bufb, sem, *, tt):
    base = pl.program_id(0) * tt

    def copy_a(t, slot):
        return pltpu.make_async_copy(y_ref.at[slot], bufa.at[t], sem.at[0])

    def copy_b(t, slot):
        return pltpu.make_async_copy(y_ref.at[slot], bufb.at[t], sem.at[1])

    def issue(t, c):
        copy_a(t, da_ref[base + t]).start()
        copy_b(t, db_ref[base + t]).start()
        return c

    lax.fori_loop(0, tt, issue, 0)

    def drain(t, c):
        copy_a(0, 0).wait()
        copy_b(0, 0).wait()
        return c

    lax.fori_loop(0, tt, drain, 0)
    o_ref[...] = x_ref[...] + wa_ref[...] * bufa[...] + wb_ref[...] * bufb[...]


def _combine(x1, w_a, w_b, y, dest_a, dest_b, tt=256):
    s, d = x1.shape
    sub = d // LANES
    x3 = x1.reshape(s, sub, LANES)
    y3 = y.reshape(y.shape[0], sub, LANES)
    tok = pl.BlockSpec((tt, sub, LANES), lambda i, da, db: (i, 0, 0))
    wsp = pl.BlockSpec((tt, 1, LANES), lambda i, da, db: (i, 0, 0))
    grid_spec = pltpu.PrefetchScalarGridSpec(
        num_scalar_prefetch=2,
        grid=(s // tt,),
        in_specs=[tok, wsp, wsp, pl.BlockSpec(memory_space=pl.ANY)],
        out_specs=tok,
        scratch_shapes=[pltpu.VMEM((tt, sub, LANES), F32), pltpu.VMEM((tt, sub, LANES), F32),
                        pltpu.SemaphoreType.DMA((2,))],
    )
    out = pl.pallas_call(
        functools.partial(_combine_kernel, tt=tt),
        grid_spec=grid_spec,
        out_shape=jax.ShapeDtypeStruct(x3.shape, F32),
        compiler_params=_cparams(("arbitrary",)),
    )(dest_a, dest_b, x3, w_a.reshape(s, 1, LANES), w_b.reshape(s, 1, LANES), y3)
    return out.reshape(s, d)


def _moe(x1, gain, w_gr, b_gr, w_er, b_er, w_gate, w_up, w_down):
    s, d = x1.shape
    pad = LANES - MOE_GROUPS - MOE_EXPERTS
    w_router = jnp.concatenate([w_gr, w_er, jnp.zeros((d, pad), F32)], axis=1)
    b_router = jnp.concatenate([b_gr, b_er, jnp.zeros((pad,), F32)]).reshape(1, LANES)
    h2, route, w_a, w_b = _router(x1, gain, w_router, b_router)
    rank, cnt = _rank(route)
    counts = cnt[0, :MOE_EXPERTS].astype(jnp.int32)
    padded = (counts + MOE_BLOCK - 1) // MOE_BLOCK * MOE_BLOCK
    padded_ends = jnp.cumsum(padded)
    padded_starts = padded_ends - padded
    e_a = route[:, 0].astype(jnp.int32)
    e_b = route[:, 1].astype(jnp.int32)
    dest_a = padded_starts[e_a] + rank[:, 0].astype(jnp.int32)
    dest_b = padded_starts[e_b] + rank[:, 1].astype(jnp.int32)
    n_blocks = -(-(2 * s) // MOE_BLOCK) + MOE_EXPERTS
    n_slots = n_blocks * MOE_BLOCK
    block_expert = jnp.clip(
        jnp.searchsorted(padded_ends, jnp.arange(n_blocks, dtype=jnp.int32) * MOE_BLOCK, side="right"),
        0, MOE_EXPERTS - 1).astype(jnp.int32)
    n_used = (padded_ends[-1:] // MOE_BLOCK).astype(jnp.int32)
    xs = _dispatch(h2, dest_a, dest_b, n_slots)
    y = _experts(xs, block_expert, n_used, w_gate, w_up, w_down)
    return _combine(x1, w_a, w_b, y, dest_a, dest_b)


def _alibi_slopes():
    n = ATT_GROUPS * ATT_HEADS
    s = jnp.exp2(-ALIBI_MAX_BIAS * jnp.arange(1, n + 1, dtype=F32) / n)
    return s.reshape(ATT_GROUPS, ATT_HEADS)


def _layer(x, norm1_gain, w_in, q_norm_gain, k_norm_gain, gdn_conv_w, gdn_a_log, gdn_dt_bias,
           gdn_norm_gain, w_branch_att, w_branch_gdn, w_out, norm2_gain, w_group_router,
           b_group_router, w_expert_router, b_expert_router, w_gate, w_up, w_down):
    s, d = x.shape
    h = _rmsnorm(x, norm1_gain)

    qg = jnp.broadcast_to(q_norm_gain[:, None, :] * (HEAD_DIM ** -0.5), (ATT_GROUPS, ATT_HEADS, HEAD_DIM))
    kg = jnp.broadcast_to(k_norm_gain[:, None, :], (ATT_GROUPS, ATT_HEADS, HEAD_DIM))
    qk_gain = jnp.concatenate([qg.reshape(-1), kg.reshape(-1)]).reshape(1, QK_COLS)

    qk_heads = _inproj(h, w_in, 0, QK_COLS, "qknorm", gain=qk_gain)
    heads = _inproj(h, w_in, QK_COLS, HEADS_COLS, "heads")
    ab = _inproj(h, w_in, AB_OFF, AB_COLS, "f32", tn=AB_COLS)
    gates = _inproj(h, w_in[:, GATE_OFF:], 0, GATE_COLS, "sigmoid")

    slopes = _alibi_slopes()
    outs, lses = [], []
    for g, (window, dil) in enumerate(ATT_PATTERNS):
        o, lse = _attn_group(qk_heads, heads, slopes[g], g, window, dil)
        outs.append(o)
        lses.append(lse)
    y_att = _attn_combine(outs, lses)

    gq = _gdn_conv(heads, gdn_conv_w, ATT_HEADS)
    gb = ab.reshape(s, 2, 2, GDN_V_HEADS).transpose(3, 0, 1, 2).reshape(GDN_V_HEADS, s, 4)
    prm = jnp.concatenate([gdn_a_log, gdn_dt_bias], axis=0).T.reshape(GDN_V_HEADS, 1, 4)
    z_first = ATT_HEADS + 2 * GDN_QK_HEADS + GDN_V_HEADS
    y_gdn = _gdn(gq, heads, z_first, gb, prm, gdn_norm_gain.reshape(1, HEAD_DIM))

    merged = _branch(y_att, y_gdn, w_branch_att, w_branch_gdn, gates)
    x1 = _outproj(merged, w_out, x)
    return _moe(x1, norm2_gain, w_group_router, b_group_router, w_expert_router, b_expert_router,
                w_gate, w_up, w_down)


def kernel(x, norm1_gain, w_in, q_norm_gain, k_norm_gain, gdn_conv_w, gdn_a_log, gdn_dt_bias,
           gdn_norm_gain, w_branch_att, w_branch_gdn, w_out, norm2_gain, w_group_router,
           b_group_router, w_expert_router, b_expert_router, w_gate, w_up, w_down):
    b, s, d = x.shape
    params = (norm1_gain, w_in, q_norm_gain, k_norm_gain, gdn_conv_w, gdn_a_log, gdn_dt_bias,
              gdn_norm_gain, w_branch_att, w_branch_gdn, w_out, norm2_gain, w_group_router,
              b_group_router, w_expert_router, b_expert_router, w_gate, w_up, w_down)
    outs = []
    for bi in range(b):
        xb = x[bi]
        for i in range(norm1_gain.shape[0]):
            xb = _layer(xb, *(p[i] for p in params))
        outs.append(xb)
    return jnp.stack(outs, axis=0)
```

```python
import functools

import jax
import jax.numpy as jnp
from jax import lax
from jax.experimental import pallas as pl
from jax.experimental.pallas import tpu as pltpu

F32 = jnp.float32
BF16 = jnp.bfloat16
HIGHEST = lax.Precision.HIGHEST

NORM_EPS = 1e-6
D_MODEL = 2048
HEAD_DIM = 128
ATT_HEADS = 16
ATT_PATTERNS = ((128, 1), (512, 4), (2048, 16))
ATT_GROUPS = len(ATT_PATTERNS)
ALIBI_MAX_BIAS = 8.0
GDN_QK_HEADS = 16
GDN_V_HEADS = 32
GDN_CONV_WIDTH = 5
GDN_CHUNK = 64
MOE_GROUPS = 8
MOE_EXPERTS_PER_GROUP = 8
MOE_EXPERTS = 64
MOE_FF = 512
MOE_BLOCK = 128

ATT_W = ATT_HEADS * HEAD_DIM
QK_COLS = 2 * ATT_GROUPS * ATT_W
HEADS_COLS = ATT_W + 2 * GDN_QK_HEADS * HEAD_DIM + 2 * GDN_V_HEADS * HEAD_DIM
AB_OFF = QK_COLS + HEADS_COLS
AB_COLS = 4 * GDN_V_HEADS
GATE_OFF = AB_OFF + AB_COLS
GATE_COLS = 2 * D_MODEL

VMEM_LIMIT = 56 * 1024 * 1024
LANES = 128


def _cparams(sem, vmem=VMEM_LIMIT):
    return pltpu.CompilerParams(dimension_semantics=sem, vmem_limit_bytes=vmem)


def _sigmoid(x):
    return 1.0 / (1.0 + jnp.exp(-x))


def _softplus(x):
    return jnp.maximum(x, 0.0) + jnp.log(1.0 + jnp.exp(-jnp.abs(x)))


def _rmsnorm_kernel(x_ref, g_ref, o_ref):
    x = x_ref[...]
    ms = jnp.mean(x * x, axis=-1, keepdims=True)
    o_ref[...] = (x * lax.rsqrt(ms + NORM_EPS) * g_ref[...]).astype(o_ref.dtype)


def _rmsnorm(x, gain, tm=512):
    s, d = x.shape
    return pl.pallas_call(
        _rmsnorm_kernel,
        grid=(s // tm,),
        in_specs=[pl.BlockSpec((tm, d), lambda i: (i, 0)), pl.BlockSpec((1, d), lambda i: (0, 0))],
        out_specs=pl.BlockSpec((tm, d), lambda i: (i, 0)),
        out_shape=jax.ShapeDtypeStruct((s, d), BF16),
        compiler_params=_cparams(("parallel",)),
    )(x, gain.reshape(1, d))


def _inproj_kernel(*refs, mode, tm, tn, dils):
    refs = list(refs)
    x_ref, w_ref = refs[:2]
    pos = 2
    g_ref = None
    if mode == "qknorm":
        g_ref = refs[pos]
        pos += 1
    n_out = len(dils) if mode in ("qknorm", "heads") else 1
    o_refs = refs[pos:pos + n_out]
    wb_ref = refs[pos + n_out]
    dl_ref = refs[pos + n_out + 1] if max(dils) > 1 else None

    @pl.when(pl.program_id(1) == 0)
    def _():
        wb_ref[...] = w_ref[...].astype(BF16)

    acc = jnp.dot(x_ref[...], wb_ref[...], preferred_element_type=F32)
    if mode in ("qknorm", "heads"):
        for c in range(tn // HEAD_DIM):
            a = acc[:, c * HEAD_DIM:(c + 1) * HEAD_DIM]
            if mode == "qknorm":
                ms = jnp.mean(a * a, axis=-1, keepdims=True)
                a = a * lax.rsqrt(ms + NORM_EPS) * g_ref[:, c * HEAD_DIM:(c + 1) * HEAD_DIM]
            if dl_ref is not None:
                dl_ref[...] = a
            for o_ref, dil in zip(o_refs, dils):
                if dil == 1:
                    o_ref[c] = a.astype(o_ref.dtype)
                else:
                    for r in range(dil):
                        o_ref[c, r] = dl_ref[pl.ds(r, tm // dil, stride=dil), :].astype(o_ref.dtype)
    elif mode == "f32":
        o_refs[0][...] = acc
    else:
        o_refs[0][...] = _sigmoid(acc).astype(o_refs[0].dtype)


def _inproj(h, w, col_block, n_tiles, mode, gain=None, gain_block=None, dils=(1,), tm=1024, tn=1024):
    s, d = h.shape
    assert s % tm == 0
    in_specs = [pl.BlockSpec((tm, d), lambda j, i: (i, 0)),
                pl.BlockSpec((d, tn), lambda j, i: (0, col_block(j)))]
    args = [h, w]
    if mode == "qknorm":
        in_specs.append(pl.BlockSpec((1, tn), lambda j, i: (0, gain_block(j))))
        args.append(gain)
    scratch = [pltpu.VMEM((d, tn), BF16)]
    if mode in ("qknorm", "heads"):
        hpt = tn // HEAD_DIM
        n_heads = n_tiles * hpt
        out_specs, out_shape = [], []
        for dil in dils:
            if dil == 1:
                out_specs.append(pl.BlockSpec((hpt, tm, HEAD_DIM), lambda j, i: (j, i, 0)))
                out_shape.append(jax.ShapeDtypeStruct((n_heads, s, HEAD_DIM), BF16))
            else:
                assert tm % (16 * dil) == 0
                out_specs.append(pl.BlockSpec((hpt, dil, tm // dil, HEAD_DIM), lambda j, i: (j, 0, i, 0)))
                out_shape.append(jax.ShapeDtypeStruct((n_heads, dil, s // dil, HEAD_DIM), BF16))
        if max(dils) > 1:
            scratch.append(pltpu.VMEM((tm, HEAD_DIM), F32))
    else:
        out_specs = [pl.BlockSpec((tm, tn), lambda j, i: (i, j))]
        out_shape = [jax.ShapeDtypeStruct((s, n_tiles * tn), F32 if mode == "f32" else BF16)]
    outs = pl.pallas_call(
        functools.partial(_inproj_kernel, mode=mode, tm=tm, tn=tn, dils=tuple(dils)),
        grid=(n_tiles, s // tm),
        in_specs=in_specs,
        out_specs=out_specs,
        out_shape=out_shape,
        scratch_shapes=scratch,
        compiler_params=_cparams(("parallel", "arbitrary")),
    )(*args)
    return outs if len(outs) > 1 else outs[0]


def _attn_kernel(slope_ref, q_ref, k_ref, v_ref, o_ref, lse_ref, *, length, tq, half):
    slope = slope_ref[pl.program_id(0)]
    win = tq + 2 * half
    base = (lax.broadcasted_iota(jnp.int32, (tq, win), 0)
            - lax.broadcasted_iota(jnp.int32, (tq, win), 1))
    eye = (lax.broadcasted_iota(jnp.int32, (tq, tq), 0)
           == lax.broadcasted_iota(jnp.int32, (tq, tq), 1))

    def body(i, carry):
        q0 = pl.multiple_of(i * tq, tq)
        start = pl.multiple_of(jnp.clip(q0 - half, 0, length - win), half)
        q = q_ref[pl.ds(q0, tq), :]
        k = k_ref[pl.ds(start, win), :]
        v = v_ref[pl.ds(start, win), :]
        s = lax.dot_general(q, k, (((1,), (1,)), ((), ())), preferred_element_type=F32)
        dist = jnp.abs(base + (q0 - start))
        s = s - slope * dist.astype(F32)
        s = jnp.where(dist <= half, s, -1e30)
        m = jnp.max(s, axis=-1, keepdims=True)
        p = jnp.exp(s - m)
        l = jnp.sum(p, axis=-1, keepdims=True)
        pv = jnp.dot(p.astype(BF16), v, preferred_element_type=F32)
        o_ref[pl.ds(q0, tq), :] = (pv / l).astype(o_ref.dtype)
        lse = m + jnp.log(l)
        lse_ref[pl.ds(i, 1), :] = jnp.sum(jnp.where(eye, lse, 0.0), axis=0, keepdims=True)
        return carry

    lax.fori_loop(0, length // tq, body, 0)


def _attn_group(qk, v, slopes, window, dil, tq=128):
    _, _, length, c = qk.shape
    half = window // (2 * dil)
    assert length >= tq + 2 * half and length % tq == 0
    n_t = length // tq
    sub = lambda off: pl.BlockSpec((None, None, length, c), lambda h, r, sl: (off + h, r, 0, 0))
    grid_spec = pltpu.PrefetchScalarGridSpec(
        num_scalar_prefetch=1,
        grid=(ATT_HEADS, dil),
        in_specs=[sub(0), sub(ATT_HEADS), sub(0)],
        out_specs=[sub(0), pl.BlockSpec((None, None, n_t, tq), lambda h, r, sl: (h, r, 0, 0))],
    )
    o, lse = pl.pallas_call(
        functools.partial(_attn_kernel, length=length, tq=tq, half=half),
        grid_spec=grid_spec,
        out_shape=[jax.ShapeDtypeStruct((ATT_HEADS, dil, length, c), BF16),
                   jax.ShapeDtypeStruct((ATT_HEADS, dil, n_t, tq), F32)],
        compiler_params=_cparams(("parallel", "parallel")),
    )(slopes * float(dil), qk, qk, v)
    lse = lse.reshape(ATT_HEADS, dil, length).transpose(2, 1, 0).reshape(length * dil, ATT_HEADS)
    return o, lse


def _attn_combine_kernel(o1_ref, o2_ref, o3_ref, l1_ref, l2_ref, l3_ref, y_ref, il_ref, *, ts, dils):
    l1, l2, l3 = l1_ref[...], l2_ref[...], l3_ref[...]
    m = jnp.maximum(jnp.maximum(l1, l2), l3)
    e1, e2, e3 = jnp.exp(l1 - m), jnp.exp(l2 - m), jnp.exp(l3 - m)
    den = e1 + e2 + e3
    ws = (e1 / den, e2 / den, e3 / den)
    for h in range(ATT_HEADS):
        y = jnp.zeros((ts, HEAD_DIM), F32)
        for o_ref, w, dil in zip((o1_ref, o2_ref, o3_ref), ws, dils):
            if dil == 1:
                og = o_ref[h, 0].astype(F32)
            else:
                for r in range(dil):
                    il_ref[pl.ds(r, ts // dil, stride=dil), :] = o_ref[h, r].astype(F32)
                og = il_ref[...]
            y = y + w[:, h:h + 1] * og
        y_ref[:, h * HEAD_DIM:(h + 1) * HEAD_DIM] = y.astype(y_ref.dtype)


def _attn_combine(outs, lses, dils, ts=512):
    s = lses[0].shape[0]
    c = HEAD_DIM
    ospecs = [pl.BlockSpec((ATT_HEADS, dil, ts // dil, c), lambda i: (0, 0, i, 0)) for dil in dils]
    lspec = pl.BlockSpec((ts, ATT_HEADS), lambda i: (i, 0))
    return pl.pallas_call(
        functools.partial(_attn_combine_kernel, ts=ts, dils=tuple(dils)),
        grid=(s // ts,),
        in_specs=ospecs + [lspec] * 3,
        out_specs=pl.BlockSpec((ts, ATT_W), lambda i: (i, 0)),
        out_shape=jax.ShapeDtypeStruct((s, ATT_W), BF16),
        scratch_shapes=[pltpu.VMEM((ts, c), F32)],
        compiler_params=_cparams(("parallel",)),
    )(*outs, *lses)


def _gdn_conv_kernel(x_ref, p_ref, n_ref, w_ref, o_ref, *, tr, halo):
    c = pl.program_id(0)
    i = pl.program_id(1)
    x = x_ref[...].astype(F32)
    prev = jnp.where(i > 0, p_ref[...].astype(F32), 0.0)
    nxt = jnp.where(i < pl.num_programs(1) - 1, n_ref[...].astype(F32), 0.0)
    xc = jnp.concatenate([prev, x, nxt], axis=0)
    n = tr + 2 * halo
    w = w_ref[...]
    y = jnp.zeros((tr, HEAD_DIM), F32)
    for j in range(GDN_CONV_WIDTH):
        shift = (GDN_CONV_WIDTH - 1) // 2 - j
        xs = xc if shift == 0 else pltpu.roll(xc, shift % n, 0)
        y = y + xs[halo:halo + tr] * w[j:j + 1]
    y = y * _sigmoid(y)
    nrm = lax.rsqrt(jnp.sum(y * y, axis=-1, keepdims=True) + NORM_EPS)
    scale = jnp.where(c < GDN_QK_HEADS, nrm * (HEAD_DIM ** -0.5),
                      jnp.where(c < 2 * GDN_QK_HEADS, nrm, 1.0))
    o_ref[...] = (y * scale).astype(o_ref.dtype)


def _gdn_conv(heads, conv_w, first_head, tr=2048, halo=16):
    _, s, c = heads.shape
    n_heads = 2 * GDN_QK_HEADS + GDN_V_HEADS
    hb = tr // halo
    n_hb = s // halo
    return pl.pallas_call(
        functools.partial(_gdn_conv_kernel, tr=tr, halo=halo),
        grid=(n_heads, s // tr),
        in_specs=[
            pl.BlockSpec((None, tr, c), lambda h, i: (first_head + h, i, 0)),
            pl.BlockSpec((None, halo, c), lambda h, i: (first_head + h, jnp.maximum(i * hb - 1, 0), 0)),
            pl.BlockSpec((None, halo, c), lambda h, i: (first_head + h, jnp.minimum((i + 1) * hb, n_hb - 1), 0)),
            pl.BlockSpec((GDN_CONV_WIDTH, c), lambda h, i: (0, h)),
        ],
        out_specs=pl.BlockSpec((None, tr, c), lambda h, i: (h, i, 0)),
        out_shape=jax.ShapeDtypeStruct((n_heads, s, c), BF16),
        compiler_params=_cparams(("parallel", "parallel")),
    )(heads, heads, heads, conv_w)


GDN_SEG = 32
GDN_CHAINS = 4
GDN_PREP_UNROLL = 2


def _gdn_kernel(q_ref, k_ref, v_ref, z_ref, gb_ref, prm_ref, ng_ref, o_ref,
                acc_ref, a_scr, b_scr, qp_scr, d_scr, s_scr, *, seq, chunk, seg):
    n_seg = seq // (seg * chunk)
    prm = prm_ref[...]
    ii = lax.broadcasted_iota(jnp.int32, (chunk, chunk), 0)
    jj = lax.broadcasted_iota(jnp.int32, (chunk, chunk), 1)
    eye = ii == jj
    row = lax.broadcasted_iota(jnp.int32, (chunk, LANES), 0)
    steps = chunk.bit_length() - 1
    acc_ref[...] = jnp.zeros_like(acc_ref)
    s_scr[...] = jnp.zeros_like(s_scr)

    def seg_of(si, d):
        return si if d == 0 else n_seg - 1 - si

    def prep_chunks(si, nls):
        nt = (((1,), (1,)), ((), ()))
        grams = []
        for nl in nls:
            for d in range(2):
                r0 = pl.multiple_of((seg_of(si, d) * seg + nl) * chunk, chunk)
                qb = q_ref[pl.ds(r0, chunk), :]
                kb = k_ref[pl.ds(r0, chunk), :]
                kk = lax.dot_general(kb, kb, nt, preferred_element_type=F32)
                qk = lax.dot_general(qb, kb, nt, preferred_element_type=F32)
                grams.append((nl, d, r0, qb, kb, kk, qk))
        chains = []
        for nl, d, r0, qb, kb, kk, qk in grams:
            qc, kc = qb.astype(F32), kb.astype(F32)
            gbt = gb_ref[pl.ds(r0, chunk), :]
            incl = (ii >= jj) if d == 0 else (ii <= jj)
            strict = (ii > jj) if d == 0 else (ii < jj)
            for vh in range(2):
                ca, cb = 4 * vh + d, 4 * vh + 2 + d
                vc = v_ref[vh, pl.ds(r0, chunk), :].astype(F32)
                g = -jnp.exp(prm[:, ca:ca + 1]) * _softplus(gbt[:, ca:ca + 1] + prm[:, cb:cb + 1])
                beta = jnp.broadcast_to(_sigmoid(gbt[:, cb:cb + 1]), (chunk, LANES))
                gc = jnp.broadcast_to(g, (chunk, LANES))
                sh = 1
                while sh < chunk:
                    if d == 0:
                        gc = gc + jnp.where(row >= sh, pltpu.roll(gc, sh, 0), 0.0)
                    else:
                        gc = gc + jnp.where(row < chunk - sh, pltpu.roll(gc, chunk - sh, 0), 0.0)
                    sh *= 2
                tot = gc[chunk - 1:chunk, :] if d == 0 else gc[0:1, :]
                gcc = gc[:, :chunk]
                gc_row = jnp.sum(jnp.where(eye, gcc, 0.0), axis=0, keepdims=True)
                decay = jnp.exp(jnp.where(incl, gcc - gc_row, 0.0))
                eg = jnp.exp(gc)
                chains.append(dict(
                    c=2 * d + vh, vh=vh, nl=nl, r0=r0,
                    p=-jnp.where(strict, beta[:, :chunk] * kk * decay, 0.0),
                    x=jnp.concatenate([vc * beta, kc * (beta * eg)], axis=1),
                    intra=jnp.where(incl, qk * decay, 0.0).astype(BF16),
                    k_tail=(kc * jnp.exp(tot - gc)).astype(BF16),
                    q_dec=qc * eg,
                    dn=jnp.broadcast_to(jnp.exp(tot), (8, LANES))))
        for t in range(steps):
            for ch in chains:
                pb = ch["p"].astype(BF16)
                ch["x"] = ch["x"] + jnp.dot(pb, ch["x"].astype(BF16), preferred_element_type=F32)
                if t < steps - 1:
                    ch["p"] = jnp.dot(pb, pb, preferred_element_type=F32)
        for ch in chains:
            xb = ch["x"].astype(BF16)
            ch["kx"] = lax.dot_general(ch["k_tail"], xb, (((0,), (0,)), ((), ())),
                                       preferred_element_type=F32)
            ch["ix"] = jnp.dot(ch["intra"], xb, preferred_element_type=F32)
        for ch in chains:
            c, nl, kx, ix = ch["c"], ch["nl"], ch["kx"], ch["ix"]
            b_scr[c, nl] = kx[:, :HEAD_DIM]
            a_scr[c, nl] = (-kx[:, HEAD_DIM:]).astype(BF16)
            qp_scr[c, pl.ds(pl.multiple_of(nl * chunk, chunk), chunk), :] = (
                ch["q_dec"] - ix[:, HEAD_DIM:]).astype(BF16)
            d_scr[c, nl] = ch["dn"]
            acc_ref[ch["vh"], pl.ds(ch["r0"], chunk), :] += ix[:, :HEAD_DIM]


    def scan_chunk(si, i):
        work = []
        for d in range(2):
            nl = i if d == 0 else seg - 1 - i
            r0 = pl.multiple_of((seg_of(si, d) * seg + nl) * chunk, chunk)
            l0 = pl.multiple_of(nl * chunk, chunk)
            for vh in range(2):
                c = 2 * d + vh
                state = s_scr[c]
                lhs = jnp.concatenate([a_scr[c, nl], qp_scr[c, pl.ds(l0, chunk), :]], axis=0)
                r = jnp.dot(lhs, state.astype(BF16), preferred_element_type=F32)
                work.append((c, vh, nl, r0, state, r))
        for c, vh, nl, r0, state, r in work:
            acc_ref[vh, pl.ds(r0, chunk), :] += r[HEAD_DIM:]
            s_scr[c] = state * d_scr[c, nl][0:1, :] + r[:HEAD_DIM] + b_scr[c, nl]

    def segment(si, carry):
        def prep(it, cr):
            prep_chunks(si, [it * GDN_PREP_UNROLL + u for u in range(GDN_PREP_UNROLL)])
            return cr

        lax.fori_loop(0, seg // GDN_PREP_UNROLL, prep, 0)

        def scan(i, cr):
            scan_chunk(si, i)
            return cr

        lax.fori_loop(0, seg, scan, 0)
        return carry

    lax.fori_loop(0, n_seg, segment, 0)

    tile = 256

    def fin(i, carry):
        r0 = pl.multiple_of(i * tile, tile)
        for vh in range(2):
            o = acc_ref[vh, pl.ds(r0, tile), :]
            z = z_ref[vh, pl.ds(r0, tile), :].astype(F32)
            ms = jnp.mean(o * o, axis=-1, keepdims=True)
            y = o * lax.rsqrt(ms + NORM_EPS) * ng_ref[...] * (z * _sigmoid(z))
            o_ref[pl.ds(r0, tile), vh * HEAD_DIM:(vh + 1) * HEAD_DIM] = y.astype(o_ref.dtype)
        return carry

    lax.fori_loop(0, seq // tile, fin, 0)


def _gdn(gq, heads, z_first, gb, prm, norm_gain):
    _, s, c = gq.shape
    seg = GDN_SEG
    assert s % (seg * GDN_CHUNK) == 0 and z_first % 2 == 0
    once = pl.Buffered(1)
    return pl.pallas_call(
        functools.partial(_gdn_kernel, seq=s, chunk=GDN_CHUNK, seg=seg),
        grid=(GDN_QK_HEADS,),
        in_specs=[
            pl.BlockSpec((None, s, c), lambda h: (h, 0, 0), pipeline_mode=once),
            pl.BlockSpec((None, s, c), lambda h: (GDN_QK_HEADS + h, 0, 0), pipeline_mode=once),
            pl.BlockSpec((2, s, c), lambda h: (GDN_QK_HEADS + h, 0, 0), pipeline_mode=once),
            pl.BlockSpec((2, s, c), lambda h: (z_first // 2 + h, 0, 0), pipeline_mode=once),
            pl.BlockSpec((None, s, 8), lambda h: (h, 0, 0), pipeline_mode=once),
            pl.BlockSpec((None, 1, 8), lambda h: (h, 0, 0)),
            pl.BlockSpec((1, c), lambda h: (0, 0)),
        ],
        out_specs=pl.BlockSpec((s, 2 * c), lambda h: (0, h)),
        out_shape=jax.ShapeDtypeStruct((s, GDN_V_HEADS * c), BF16),
        scratch_shapes=[
            pltpu.VMEM((2, s, c), F32),
            pltpu.VMEM((GDN_CHAINS, seg, c, c), BF16),
            pltpu.VMEM((GDN_CHAINS, seg, c, c), F32),
            pltpu.VMEM((GDN_CHAINS, seg * GDN_CHUNK, c), BF16),
            pltpu.VMEM((GDN_CHAINS, seg, 8, LANES), F32),
            pltpu.VMEM((GDN_CHAINS, c, c), F32),
        ],
        compiler_params=_cparams(("parallel",)),
    )(gq, gq, gq, heads, gb, prm, norm_gain)


def _branch_kernel(ya_ref, yg_ref, wa_ref, wg_ref, ga_ref, gg_ref, o_ref, wab_ref, wgb_ref):
    @pl.when(pl.program_id(1) == 0)
    def _():
        wab_ref[...] = wa_ref[...].astype(BF16)
        wgb_ref[...] = wg_ref[...].astype(BF16)

    a = jnp.dot(ya_ref[...], wab_ref[...], preferred_element_type=F32)
    g = jnp.dot(yg_ref[...], wgb_ref[...], preferred_element_type=F32)
    o_ref[...] = (ga_ref[...].astype(F32) * a + gg_ref[...].astype(F32) * g).astype(o_ref.dtype)


def _branch(y_att, y_gdn, w_a, w_g, gates, tm=1024, tn=256):
    s, da = y_att.shape
    dg = y_gdn.shape[1]
    n = w_a.shape[1]
    goff = n // tn
    return pl.pallas_call(
        _branch_kernel,
        grid=(n // tn, s // tm),
        in_specs=[
            pl.BlockSpec((tm, da), lambda j, i: (i, 0)),
            pl.BlockSpec((tm, dg), lambda j, i: (i, 0)),
            pl.BlockSpec((da, tn), lambda j, i: (0, j)),
            pl.BlockSpec((dg, tn), lambda j, i: (0, j)),
            pl.BlockSpec((tm, tn), lambda j, i: (i, j)),
            pl.BlockSpec((tm, tn), lambda j, i: (i, goff + j)),
        ],
        out_specs=pl.BlockSpec((tm, tn), lambda j, i: (i, j)),
        out_shape=jax.ShapeDtypeStruct((s, n), BF16),
        scratch_shapes=[pltpu.VMEM((da, tn), BF16), pltpu.VMEM((dg, tn), BF16)],
        compiler_params=_cparams(("parallel", "arbitrary")),
    )(y_att, y_gdn, w_a, w_g, gates, gates)


def _outproj_kernel(m_ref, w_ref, x_ref, o_ref, wb_ref):
    @pl.when(pl.program_id(1) == 0)
    def _():
        wb_ref[...] = w_ref[...].astype(BF16)

    o_ref[...] = x_ref[...] + jnp.dot(m_ref[...], wb_ref[...], preferred_element_type=F32)


def _outproj(merged, w, x, tm=1024, tn=512):
    s, d = merged.shape
    n = w.shape[1]
    return pl.pallas_call(
        _outproj_kernel,
        grid=(n // tn, s // tm),
        in_specs=[
            pl.BlockSpec((tm, d), lambda j, i: (i, 0)),
            pl.BlockSpec((d, tn), lambda j, i: (0, j)),
            pl.BlockSpec((tm, tn), lambda j, i: (i, j)),
        ],
        out_specs=pl.BlockSpec((tm, tn), lambda j, i: (i, j)),
        out_shape=jax.ShapeDtypeStruct((s, n), F32),
        scratch_shapes=[pltpu.VMEM((d, tn), BF16)],
        compiler_params=_cparams(("parallel", "arbitrary")),
    )(merged, w, x)


def _router_kernel(x_ref, g_ref, wr_ref, br_ref, h_ref, route_ref, wa_ref, wb_ref):
    x = x_ref[...]
    ms = jnp.mean(x * x, axis=-1, keepdims=True)
    h = x * lax.rsqrt(ms + NORM_EPS) * g_ref[...]
    h_ref[...] = h.astype(h_ref.dtype)
    logits = jnp.dot(h, wr_ref[...], precision=HIGHEST, preferred_element_type=F32) + br_ref[...]
    lane_i = lax.broadcasted_iota(jnp.int32, logits.shape, 1)
    lane = lane_i.astype(F32)
    neg = -1e30
    big = 1e6
    is_g = lane_i < MOE_GROUPS
    lg = jnp.where(is_g, logits, neg)
    mg = jnp.max(lg, axis=-1, keepdims=True)
    sg = jnp.sum(jnp.where(is_g, jnp.exp(lg - mg), 0.0), axis=-1, keepdims=True)
    group_w = 1.0 / sg
    gid = jnp.min(jnp.where(is_g & (lg == mg), lane, big), axis=-1, keepdims=True)
    e_lane = lane_i - MOE_GROUPS
    in_grp = (e_lane >= 0) & (e_lane < MOE_EXPERTS) & ((e_lane // MOE_EXPERTS_PER_GROUP).astype(F32) == gid)
    le = jnp.where(in_grp, logits, neg)
    m1 = jnp.max(le, axis=-1, keepdims=True)
    i1 = jnp.min(jnp.where(in_grp & (le == m1), lane, big), axis=-1, keepdims=True)
    rest = in_grp & (lane != i1)
    le2 = jnp.where(rest, logits, neg)
    m2 = jnp.max(le2, axis=-1, keepdims=True)
    i2 = jnp.min(jnp.where(rest & (le2 == m2), lane, big), axis=-1, keepdims=True)
    se = jnp.sum(jnp.where(in_grp, jnp.exp(le - m1), 0.0), axis=-1, keepdims=True)
    p1 = 1.0 / se
    p2 = jnp.exp(m2 - m1) / se
    den = p1 + p2
    w1 = group_w * (p1 / den)
    w2 = group_w * (p2 / den)
    e1 = i1 - MOE_GROUPS
    e2 = i2 - MOE_GROUPS
    route_ref[...] = jnp.where(lane_i == 0, e1, jnp.where(lane_i == 1, e2, 0.0))
    wa_ref[...] = jnp.broadcast_to(w1, wa_ref.shape)
    wb_ref[...] = jnp.broadcast_to(w2, wb_ref.shape)


def _router(x1, gain, w_router, b_router, tm=512):
    s, d = x1.shape
    row = pl.BlockSpec((tm, LANES), lambda i: (i, 0))
    return pl.pallas_call(
        _router_kernel,
        grid=(s // tm,),
        in_specs=[
            pl.BlockSpec((tm, d), lambda i: (i, 0)),
            pl.BlockSpec((1, d), lambda i: (0, 0)),
            pl.BlockSpec((d, LANES), lambda i: (0, 0)),
            pl.BlockSpec((1, LANES), lambda i: (0, 0)),
        ],
        out_specs=[pl.BlockSpec((tm, d), lambda i: (i, 0)), row, row, row],
        out_shape=[jax.ShapeDtypeStruct((s, d), BF16)] + [jax.ShapeDtypeStruct((s, LANES), F32)] * 3,
        compiler_params=_cparams(("parallel",)),
    )(x1, gain.reshape(1, d), w_router, b_router)


def _rank_kernel(route_ref, rank_ref, cnt_ref, run_ref, *, tm):
    @pl.when(pl.program_id(0) == 0)
    def _():
        run_ref[...] = jnp.zeros_like(run_ref)

    r = route_ref[...]
    lane_i = lax.broadcasted_iota(jnp.int32, r.shape, 1)
    lane = lane_i.astype(F32)
    oa = (lane == r[:, 0:1]).astype(F32)
    ob = (lane == r[:, 1:2]).astype(F32)
    both = oa + ob
    tri = (lax.broadcasted_iota(jnp.int32, (tm, tm), 0)
           > lax.broadcasted_iota(jnp.int32, (tm, tm), 1)).astype(BF16)
    before = run_ref[0:1, :] + jnp.dot(tri, both.astype(BF16), preferred_element_type=F32)
    ra = jnp.sum(oa * before, axis=-1, keepdims=True)
    rb = jnp.sum(ob * before, axis=-1, keepdims=True)
    rank_ref[...] = jnp.where(lane_i == 0, ra, jnp.where(lane_i == 1, rb, 0.0))
    run_ref[0:1, :] = run_ref[0:1, :] + jnp.sum(both, axis=0, keepdims=True)
    cnt_ref[...] = run_ref[...]


def _rank(route, tm=512):
    s = route.shape[0]
    return pl.pallas_call(
        functools.partial(_rank_kernel, tm=tm),
        grid=(s // tm,),
        in_specs=[pl.BlockSpec((tm, LANES), lambda i: (i, 0))],
        out_specs=[pl.BlockSpec((tm, LANES), lambda i: (i, 0)), pl.BlockSpec((8, LANES), lambda i: (0, 0))],
        out_shape=[jax.ShapeDtypeStruct((s, LANES), F32), jax.ShapeDtypeStruct((8, LANES), F32)],
        scratch_shapes=[pltpu.VMEM((8, LANES), F32)],
        compiler_params=_cparams(("arbitrary",)),
    )(route)


def _dispatch_kernel(da_ref, db_ref, h_ref, xs_in_ref, xs_ref, sem, *, tt):
    del xs_in_ref
    base = pl.program_id(0) * tt

    def copy(t, slot):
        return pltpu.make_async_copy(h_ref.at[t], xs_ref.at[slot], sem)

    def issue(t, c):
        copy(t, da_ref[base + t]).start()
        copy(t, db_ref[base + t]).start()
        return c

    lax.fori_loop(0, tt, issue, 0)

    def drain(t, c):
        copy(0, 0).wait()
        copy(0, 0).wait()
        return c

    lax.fori_loop(0, tt, drain, 0)


def _dispatch(h2, dest_a, dest_b, n_slots, tt=512):
    s, d = h2.shape
    sub = d // LANES
    h3 = h2.reshape(s, sub, LANES)
    xs0 = jnp.zeros((n_slots, sub, LANES), h2.dtype)
    grid_spec = pltpu.PrefetchScalarGridSpec(
        num_scalar_prefetch=2,
        grid=(s // tt,),
        in_specs=[pl.BlockSpec((tt, sub, LANES), lambda i, da, db: (i, 0, 0)),
                  pl.BlockSpec(memory_space=pl.ANY)],
        out_specs=pl.BlockSpec(memory_space=pl.ANY),
        scratch_shapes=[pltpu.SemaphoreType.DMA(())],
    )
    xs = pl.pallas_call(
        functools.partial(_dispatch_kernel, tt=tt),
        grid_spec=grid_spec,
        out_shape=jax.ShapeDtypeStruct(xs0.shape, xs0.dtype),
        input_output_aliases={3: 0},
        compiler_params=_cparams(("arbitrary",)),
    )(dest_a, dest_b, h3, xs0)
    return xs.reshape(n_slots, d)


def _expert_kernel(be_ref, nu_ref, x_ref, wg_ref, wu_ref, wd_ref, y_ref, wgb_ref, wub_ref, wdb_ref):
    b = pl.program_id(0)

    @pl.when(b < nu_ref[0])
    def _():
        changed = jnp.logical_or(b == 0, be_ref[b] != be_ref[jnp.maximum(b - 1, 0)])

        @pl.when(changed)
        def _():
            wgb_ref[...] = wg_ref[...].astype(BF16)
            wub_ref[...] = wu_ref[...].astype(BF16)
            wdb_ref[...] = wd_ref[...].astype(BF16)

        x = x_ref[...]
        g = jnp.dot(x, wgb_ref[...], preferred_element_type=F32)
        u = jnp.dot(x, wub_ref[...], preferred_element_type=F32)
        mid = (g * _sigmoid(g) * u).astype(BF16)
        y_ref[...] = jnp.dot(mid, wdb_ref[...], preferred_element_type=F32)

    @pl.when(b >= nu_ref[0])
    def _():
        y_ref[...] = jnp.zeros_like(y_ref)


def _experts(xs, block_expert, n_used, w_gate, w_up, w_down, bm=MOE_BLOCK):
    n_slots, d = xs.shape
    ff = w_gate.shape[2]
    n_blocks = n_slots // bm
    blk = lambda b, be, nu: jnp.minimum(b, nu[0] - 1)
    grid_spec = pltpu.PrefetchScalarGridSpec(
        num_scalar_prefetch=2,
        grid=(n_blocks,),
        in_specs=[
            pl.BlockSpec((bm, d), lambda b, be, nu: (blk(b, be, nu), 0)),
            pl.BlockSpec((None, d, ff), lambda b, be, nu: (be[blk(b, be, nu)], 0, 0)),
            pl.BlockSpec((None, d, ff), lambda b, be, nu: (be[blk(b, be, nu)], 0, 0)),
            pl.BlockSpec((None, ff, d), lambda b, be, nu: (be[blk(b, be, nu)], 0, 0)),
        ],
        out_specs=pl.BlockSpec((bm, d), lambda b, be, nu: (b, 0)),
        scratch_shapes=[pltpu.VMEM((d, ff), BF16), pltpu.VMEM((d, ff), BF16), pltpu.VMEM((ff, d), BF16)],
    )
    return pl.pallas_call(
        _expert_kernel,
        grid_spec=grid_spec,
        out_shape=jax.ShapeDtypeStruct((n_slots, d), F32),
        compiler_params=_cparams(("arbitrary",)),
    )(block_expert, n_used, xs, w_gate, w_up, w_down)


def _combine_kernel(da_ref, db_ref, x_ref, wa_ref, wb_ref, y_ref, o_ref, bufa, bufb, sem, *, tt):
    base = pl.program_id(0) * tt

    def copy_a(t, slot):
        return pltpu.make_async_copy(y_ref.at[slot], bufa.at[t], sem.at[0])

    def copy_b(t, slot):
        return pltpu.make_async_copy(y_ref.at[slot], bufb.at[t], sem.at[1])

    def issue(t, c):
        copy_a(t, da_ref[base + t]).start()
        copy_b(t, db_ref[base + t]).start()
        return c

    lax.fori_loop(0, tt, issue, 0)

    def drain(t, c):
        copy_a(0, 0).wait()
        copy_b(0, 0).wait()
        return c

    lax.fori_loop(0, tt, drain, 0)
    o_ref[...] = x_ref[...] + wa_ref[...] * bufa[...] + wb_ref[...] * bufb[...]


def _combine(x1, w_a, w_b, y, dest_a, dest_b, tt=256):
    s, d = x1.shape
    sub = d // LANES
    x3 = x1.reshape(s, sub, LANES)
    y3 = y.reshape(y.shape[0], sub, LANES)
    tok = pl.BlockSpec((tt, sub, LANES), lambda i, da, db: (i, 0, 0))
    wsp = pl.BlockSpec((tt, 1, LANES), lambda i, da, db: (i, 0, 0))
    grid_spec = pltpu.PrefetchScalarGridSpec(
        num_scalar_prefetch=2,
        grid=(s // tt,),
        in_specs=[tok, wsp, wsp, pl.BlockSpec(memory_space=pl.ANY)],
        out_specs=tok,
        scratch_shapes=[pltpu.VMEM((tt, sub, LANES), F32), pltpu.VMEM((tt, sub, LANES), F32),
                        pltpu.SemaphoreType.DMA((2,))],
    )
    out = pl.pallas_call(
        functools.partial(_combine_kernel, tt=tt),
        grid_spec=grid_spec,
        out_shape=jax.ShapeDtypeStruct(x3.shape, F32),
        compiler_params=_cparams(("arbitrary",)),
    )(dest_a, dest_b, x3, w_a.reshape(s, 1, LANES), w_b.reshape(s, 1, LANES), y3)
    return out.reshape(s, d)


def _moe(x1, gain, w_gr, b_gr, w_er, b_er, w_gate, w_up, w_down):
    s, d = x1.shape
    pad = LANES - MOE_GROUPS - MOE_EXPERTS
    w_router = jnp.concatenate([w_gr, w_er, jnp.zeros((d, pad), F32)], axis=1)
    b_router = jnp.concatenate([b_gr, b_er, jnp.zeros((pad,), F32)]).reshape(1, LANES)
    h2, route, w_a, w_b = _router(x1, gain, w_router, b_router)
    rank, cnt = _rank(route)
    counts = cnt[0, :MOE_EXPERTS].astype(jnp.int32)
    padded = (counts + MOE_BLOCK - 1) // MOE_BLOCK * MOE_BLOCK
    padded_ends = jnp.cumsum(padded)
    padded_starts = padded_ends - padded
    e_a = route[:, 0].astype(jnp.int32)
    e_b = route[:, 1].astype(jnp.int32)
    dest_a = padded_starts[e_a] + rank[:, 0].astype(jnp.int32)
    dest_b = padded_starts[e_b] + rank[:, 1].astype(jnp.int32)
    n_blocks = -(-(2 * s) // MOE_BLOCK) + MOE_EXPERTS
    n_slots = n_blocks * MOE_BLOCK
    block_expert = jnp.clip(
        jnp.searchsorted(padded_ends, jnp.arange(n_blocks, dtype=jnp.int32) * MOE_BLOCK, side="right"),
        0, MOE_EXPERTS - 1).astype(jnp.int32)
    n_used = (padded_ends[-1:] // MOE_BLOCK).astype(jnp.int32)
    xs = _dispatch(h2, dest_a, dest_b, n_slots)
    y = _experts(xs, block_expert, n_used, w_gate, w_up, w_down)
    return _combine(x1, w_a, w_b, y, dest_a, dest_b)


def _alibi_slopes():
    n = ATT_GROUPS * ATT_HEADS
    s = jnp.exp2(-ALIBI_MAX_BIAS * jnp.arange(1, n + 1, dtype=F32) / n)
    return s.reshape(ATT_GROUPS, ATT_HEADS)


def _layer(x, norm1_gain, w_in, q_norm_gain, k_norm_gain, gdn_conv_w, gdn_a_log, gdn_dt_bias,
           gdn_norm_gain, w_branch_att, w_branch_gdn, w_out, norm2_gain, w_group_router,
           b_group_router, w_expert_router, b_expert_router, w_gate, w_up, w_down):
    s, d = x.shape
    h = _rmsnorm(x, norm1_gain)

    qg = jnp.broadcast_to(q_norm_gain[:, None, :] * (HEAD_DIM ** -0.5), (ATT_GROUPS, ATT_HEADS, HEAD_DIM))
    kg = jnp.broadcast_to(k_norm_gain[:, None, :], (ATT_GROUPS, ATT_HEADS, HEAD_DIM))
    qk_gain = jnp.concatenate([qg.reshape(-1), kg.reshape(-1)]).reshape(1, QK_COLS)

    tn = 1024
    dils = [dil for _, dil in ATT_PATTERNS]
    v_att = _inproj(h, w_in, lambda j: QK_COLS // tn + j, ATT_W // tn, "heads", dils=dils)
    v_att = [v.reshape(ATT_HEADS, dil, s // dil, HEAD_DIM) for v, dil in zip(v_att, dils)]
    heads = _inproj(h, w_in, lambda j: (QK_COLS + ATT_W) // tn + j, (HEADS_COLS - ATT_W) // tn, "heads")
    ab = _inproj(h, w_in, lambda j: AB_OFF // AB_COLS + j, 1, "f32", tn=AB_COLS)
    gates = _inproj(h, w_in[:, GATE_OFF:], lambda j: j, GATE_COLS // tn, "sigmoid")

    slopes = _alibi_slopes()
    outs, lses = [], []
    for g, (window, dil) in enumerate(ATT_PATTERNS):
        blk = lambda j, g=g: 2 * g + j + jnp.where(j >= 2, ATT_GROUPS * ATT_W // tn - 2, 0)
        qk = _inproj(h, w_in, blk, 4, "qknorm", gain=qk_gain, gain_block=blk, dils=(dil,))
        qk = qk.reshape(2 * ATT_HEADS, dil, s // dil, HEAD_DIM)
        o, lse = _attn_group(qk, v_att[g], slopes[g], window, dil)
        outs.append(o)
        lses.append(lse)
    y_att = _attn_combine(outs, lses, dils)

    gq = _gdn_conv(heads, gdn_conv_w, 0)
    gb = (ab.reshape(s, 2, 2, GDN_QK_HEADS, 2).transpose(3, 0, 4, 1, 2)
          .reshape(GDN_QK_HEADS, s, 8))
    prm = (jnp.stack([gdn_a_log, gdn_dt_bias]).reshape(2, 2, GDN_QK_HEADS, 2)
           .transpose(2, 3, 0, 1).reshape(GDN_QK_HEADS, 1, 8))
    z_first = 2 * GDN_QK_HEADS + GDN_V_HEADS
    y_gdn = _gdn(gq, heads, z_first, gb, prm, gdn_norm_gain.reshape(1, HEAD_DIM))

    merged = _branch(y_att, y_gdn, w_branch_att, w_branch_gdn, gates)
    x1 = _outproj(merged, w_out, x)
    return _moe(x1, norm2_gain, w_group_router, b_group_router, w_expert_router, b_expert_router,
                w_gate, w_up, w_down)


def kernel(x, norm1_gain, w_in, q_norm_gain, k_norm_gain, gdn_conv_w, gdn_a_log, gdn_dt_bias,
           gdn_norm_gain, w_branch_att, w_branch_gdn, w_out, norm2_gain, w_group_router,
           b_group_router, w_expert_router, b_expert_router, w_gate, w_up, w_down):
    b, s, d = x.shape
    params = (norm1_gain, w_in, q_norm_gain, k_norm_gain, gdn_conv_w, gdn_a_log, gdn_dt_bias,
              gdn_norm_gain, w_branch_att, w_branch_gdn, w_out, norm2_gain, w_group_router,
              b_group_router, w_expert_router, b_expert_router, w_gate, w_up, w_down)
    outs = []
    for bi in range(b):
        xb = x[bi]
        for i in range(norm1_gain.shape[0]):
            xb = _layer(xb, *(p[i] for p in params))
        outs.append(xb)
    return jnp.stack(outs, axis=0)
```

```python
import functools
import math

import jax
import jax.numpy as jnp
from jax import lax
from jax.experimental import pallas as pl
from jax.experimental.pallas import tpu as pltpu

F32 = jnp.float32
BF16 = jnp.bfloat16
HIGHEST = lax.Precision.HIGHEST

NORM_EPS = 1e-6
D_MODEL = 2048
HEAD_DIM = 128
ATT_HEADS = 16
ATT_PATTERNS = ((128, 1), (512, 4), (2048, 16))
ATT_GROUPS = len(ATT_PATTERNS)
ALIBI_MAX_BIAS = 8.0
GDN_QK_HEADS = 16
GDN_V_HEADS = 32
GDN_CONV_WIDTH = 5
GDN_CHUNK = 64
MOE_GROUPS = 8
MOE_EXPERTS_PER_GROUP = 8
MOE_EXPERTS = 64
MOE_FF = 512
MOE_BLOCK = 128
MOE_KEY_SHIFT = 15
MOE_KEY_STRIDE = 1 << MOE_KEY_SHIFT

ATT_W = ATT_HEADS * HEAD_DIM
QK_COLS = 2 * ATT_GROUPS * ATT_W
HEADS_COLS = ATT_W + 2 * GDN_QK_HEADS * HEAD_DIM + 2 * GDN_V_HEADS * HEAD_DIM
AB_OFF = QK_COLS + HEADS_COLS
AB_COLS = 4 * GDN_V_HEADS
GATE_OFF = AB_OFF + AB_COLS
GATE_COLS = 2 * D_MODEL

VMEM_LIMIT = 56 * 1024 * 1024
LANES = 128


def _cparams(sem, vmem=VMEM_LIMIT):
    return pltpu.CompilerParams(dimension_semantics=sem, vmem_limit_bytes=vmem)


def _sigmoid(x):
    return 1.0 / (1.0 + jnp.exp(-x))


def _softplus(x):
    return jnp.maximum(x, 0.0) + jnp.log(1.0 + jnp.exp(-jnp.abs(x)))


def _rmsnorm_kernel(x_ref, g_ref, o_ref):
    x = x_ref[...]
    ms = jnp.mean(x * x, axis=-1, keepdims=True)
    o_ref[...] = (x * lax.rsqrt(ms + NORM_EPS) * g_ref[...]).astype(o_ref.dtype)


def _rmsnorm(x, gain, tm=512):
    s, d = x.shape
    return pl.pallas_call(
        _rmsnorm_kernel,
        grid=(s // tm,),
        in_specs=[pl.BlockSpec((tm, d), lambda i: (i, 0)), pl.BlockSpec((1, d), lambda i: (0, 0))],
        out_specs=pl.BlockSpec((tm, d), lambda i: (i, 0)),
        out_shape=jax.ShapeDtypeStruct((s, d), BF16),
        compiler_params=_cparams(("parallel",)),
    )(x, gain.reshape(1, d))


def _inproj_kernel(*refs, mode, tm, tn, dils):
    refs = list(refs)
    x_ref, w_ref = refs[:2]
    pos = 2
    g_ref = None
    if mode == "qknorm":
        g_ref = refs[pos]
        pos += 1
    n_out = len(dils) if mode in ("qknorm", "heads") else 1
    o_refs = refs[pos:pos + n_out]
    wb_ref = refs[pos + n_out]
    dl_ref = refs[pos + n_out + 1] if max(dils) > 1 else None

    @pl.when(pl.program_id(1) == 0)
    def _():
        wb_ref[...] = w_ref[...].astype(BF16)

    acc = jnp.dot(x_ref[...], wb_ref[...], preferred_element_type=F32)
    if mode in ("qknorm", "heads"):
        for c in range(tn // HEAD_DIM):
            a = acc[:, c * HEAD_DIM:(c + 1) * HEAD_DIM]
            if mode == "qknorm":
                ms = jnp.mean(a * a, axis=-1, keepdims=True)
                a = a * lax.rsqrt(ms + NORM_EPS) * g_ref[:, c * HEAD_DIM:(c + 1) * HEAD_DIM]
            if dl_ref is not None:
                dl_ref[...] = a
            for o_ref, dil in zip(o_refs, dils):
                if dil == 1:
                    o_ref[c] = a.astype(o_ref.dtype)
                else:
                    for r in range(dil):
                        o_ref[c, r] = dl_ref[pl.ds(r, tm // dil, stride=dil), :].astype(o_ref.dtype)
    elif mode == "f32":
        o_refs[0][...] = acc
    else:
        o_refs[0][...] = _sigmoid(acc).astype(o_refs[0].dtype)


def _inproj(h, w, col_block, n_tiles, mode, gain=None, gain_block=None, dils=(1,), tm=1024, tn=1024):
    s, d = h.shape
    assert s % tm == 0
    in_specs = [pl.BlockSpec((tm, d), lambda j, i: (i, 0)),
                pl.BlockSpec((d, tn), lambda j, i: (0, col_block(j)))]
    args = [h, w]
    if mode == "qknorm":
        in_specs.append(pl.BlockSpec((1, tn), lambda j, i: (0, gain_block(j))))
        args.append(gain)
    scratch = [pltpu.VMEM((d, tn), BF16)]
    if mode in ("qknorm", "heads"):
        hpt = tn // HEAD_DIM
        n_heads = n_tiles * hpt
        out_specs, out_shape = [], []
        for dil in dils:
            if dil == 1:
                out_specs.append(pl.BlockSpec((hpt, tm, HEAD_DIM), lambda j, i: (j, i, 0)))
                out_shape.append(jax.ShapeDtypeStruct((n_heads, s, HEAD_DIM), BF16))
            else:
                assert tm % (16 * dil) == 0
                out_specs.append(pl.BlockSpec((hpt, dil, tm // dil, HEAD_DIM), lambda j, i: (j, 0, i, 0)))
                out_shape.append(jax.ShapeDtypeStruct((n_heads, dil, s // dil, HEAD_DIM), BF16))
        if max(dils) > 1:
            scratch.append(pltpu.VMEM((tm, HEAD_DIM), F32))
    else:
        out_specs = [pl.BlockSpec((tm, tn), lambda j, i: (i, j))]
        out_shape = [jax.ShapeDtypeStruct((s, n_tiles * tn), F32 if mode == "f32" else BF16)]
    outs = pl.pallas_call(
        functools.partial(_inproj_kernel, mode=mode, tm=tm, tn=tn, dils=tuple(dils)),
        grid=(n_tiles, s // tm),
        in_specs=in_specs,
        out_specs=out_specs,
        out_shape=out_shape,
        scratch_shapes=scratch,
        compiler_params=_cparams(("parallel", "arbitrary")),
    )(*args)
    return outs if len(outs) > 1 else outs[0]


ATT_TILES_PER_STEP = 4


def _attn_kernel(slope_ref, q_ref, k_ref, v_ref, o_ref, lse_ref, *, length, tq, half):
    slope = slope_ref[pl.program_id(0)]
    win = tq + 2 * half
    n_t = length // tq
    per_step = math.gcd(n_t, ATT_TILES_PER_STEP)
    base = (lax.broadcasted_iota(jnp.int32, (tq, win), 0)
            - lax.broadcasted_iota(jnp.int32, (tq, win), 1))
    eye = (lax.broadcasted_iota(jnp.int32, (tq, tq), 0)
           == lax.broadcasted_iota(jnp.int32, (tq, tq), 1))

    def bias_for(offset):
        dist = jnp.abs(base + offset)
        return jnp.where(dist <= half, -slope * dist.astype(F32), -1e30)

    bias_first, bias_mid, bias_last = bias_for(0), bias_for(half), bias_for(2 * half)

    def body(it, carry):
        tiles = []
        for u in range(per_step):
            i = it * per_step + u
            q0 = pl.multiple_of(i * tq, tq)
            start = pl.multiple_of(jnp.clip(q0 - half, 0, length - win), half)
            q = q_ref[pl.ds(q0, tq), :]
            k = k_ref[pl.ds(start, win), :]
            s = lax.dot_general(q, k, (((1,), (1,)), ((), ())), preferred_element_type=F32)
            tiles.append((i, q0, start, s))
        soft = []
        for i, q0, start, s in tiles:
            bias = jnp.where(i == 0, bias_first, jnp.where(i == n_t - 1, bias_last, bias_mid))
            s = s + bias
            m = jnp.max(s, axis=-1, keepdims=True)
            p = jnp.exp(s - m)
            l = jnp.sum(p, axis=-1, keepdims=True)
            soft.append((m, l, p.astype(BF16)))
        pvs = [jnp.dot(p, v_ref[pl.ds(start, win), :], preferred_element_type=F32)
               for (_, _, start, _), (_, _, p) in zip(tiles, soft)]
        for (i, q0, _, _), (m, l, _), pv in zip(tiles, soft, pvs):
            o_ref[pl.ds(q0, tq), :] = (pv / l).astype(o_ref.dtype)
            lse = m + jnp.log(l)
            lse_ref[pl.ds(i, 1), :] = jnp.sum(jnp.where(eye, lse, 0.0), axis=0, keepdims=True)
        return carry

    lax.fori_loop(0, n_t // per_step, body, 0)


def _attn_group(qk, v, slopes, window, dil, tq=128):
    _, _, length, c = qk.shape
    half = window // (2 * dil)
    assert length >= 2 * tq and length >= tq + 2 * half and length % tq == 0 and tq >= half
    n_t = length // tq
    sub = lambda off: pl.BlockSpec((None, None, length, c), lambda h, r, sl: (off + h, r, 0, 0))
    grid_spec = pltpu.PrefetchScalarGridSpec(
        num_scalar_prefetch=1,
        grid=(ATT_HEADS, dil),
        in_specs=[sub(0), sub(ATT_HEADS), sub(0)],
        out_specs=[sub(0), pl.BlockSpec((None, None, n_t, tq), lambda h, r, sl: (h, r, 0, 0))],
    )
    o, lse = pl.pallas_call(
        functools.partial(_attn_kernel, length=length, tq=tq, half=half),
        grid_spec=grid_spec,
        out_shape=[jax.ShapeDtypeStruct((ATT_HEADS, dil, length, c), BF16),
                   jax.ShapeDtypeStruct((ATT_HEADS, dil, n_t, tq), F32)],
        compiler_params=_cparams(("parallel", "parallel")),
    )(slopes * float(dil), qk, qk, v)
    lse = lse.reshape(ATT_HEADS, dil, length).transpose(2, 1, 0).reshape(length * dil, ATT_HEADS)
    return o, lse


def _attn_combine_kernel(o1_ref, o2_ref, o3_ref, l1_ref, l2_ref, l3_ref, y_ref, il_ref, *, ts, dils):
    l1, l2, l3 = l1_ref[...], l2_ref[...], l3_ref[...]
    m = jnp.maximum(jnp.maximum(l1, l2), l3)
    e1, e2, e3 = jnp.exp(l1 - m), jnp.exp(l2 - m), jnp.exp(l3 - m)
    den = e1 + e2 + e3
    ws = (e1 / den, e2 / den, e3 / den)
    for h in range(ATT_HEADS):
        y = jnp.zeros((ts, HEAD_DIM), F32)
        for o_ref, w, dil in zip((o1_ref, o2_ref, o3_ref), ws, dils):
            if dil == 1:
                og = o_ref[h, 0].astype(F32)
            else:
                for r in range(dil):
                    il_ref[pl.ds(r, ts // dil, stride=dil), :] = o_ref[h, r].astype(F32)
                og = il_ref[...]
            y = y + w[:, h:h + 1] * og
        y_ref[:, h * HEAD_DIM:(h + 1) * HEAD_DIM] = y.astype(y_ref.dtype)


def _attn_combine(outs, lses, dils, ts=512):
    s = lses[0].shape[0]
    c = HEAD_DIM
    ospecs = [pl.BlockSpec((ATT_HEADS, dil, ts // dil, c), lambda i: (0, 0, i, 0)) for dil in dils]
    lspec = pl.BlockSpec((ts, ATT_HEADS), lambda i: (i, 0))
    return pl.pallas_call(
        functools.partial(_attn_combine_kernel, ts=ts, dils=tuple(dils)),
        grid=(s // ts,),
        in_specs=ospecs + [lspec] * 3,
        out_specs=pl.BlockSpec((ts, ATT_W), lambda i: (i, 0)),
        out_shape=jax.ShapeDtypeStruct((s, ATT_W), BF16),
        scratch_shapes=[pltpu.VMEM((ts, c), F32)],
        compiler_params=_cparams(("parallel",)),
    )(*outs, *lses)


def _gdn_conv_kernel(x_ref, p_ref, n_ref, w_ref, o_ref, xc_ref, *, tr, halo):
    c = pl.program_id(0)
    i = pl.program_id(1)
    x = x_ref[...].astype(F32)
    prev = jnp.where(i > 0, p_ref[...].astype(F32), 0.0)
    nxt = jnp.where(i < pl.num_programs(1) - 1, n_ref[...].astype(F32), 0.0)
    xc_ref[0:halo, :] = prev
    xc_ref[halo:halo + tr, :] = x
    xc_ref[halo + tr:, :] = nxt
    w = w_ref[...]
    y = jnp.zeros((tr, HEAD_DIM), F32)
    for j in range(GDN_CONV_WIDTH):
        off = halo + j - (GDN_CONV_WIDTH - 1) // 2
        y = y + xc_ref[off:off + tr, :] * w[j:j + 1]
    y = y * _sigmoid(y)
    nrm = lax.rsqrt(jnp.sum(y * y, axis=-1, keepdims=True) + NORM_EPS)
    scale = jnp.where(c < GDN_QK_HEADS, nrm * (HEAD_DIM ** -0.5),
                      jnp.where(c < 2 * GDN_QK_HEADS, nrm, 1.0))
    o_ref[...] = (y * scale).astype(o_ref.dtype)


def _gdn_conv(heads, conv_w, first_head, tr=2048, halo=16):
    _, s, c = heads.shape
    n_heads = 2 * GDN_QK_HEADS + GDN_V_HEADS
    hb = tr // halo
    n_hb = s // halo
    return pl.pallas_call(
        functools.partial(_gdn_conv_kernel, tr=tr, halo=halo),
        grid=(n_heads, s // tr),
        in_specs=[
            pl.BlockSpec((None, tr, c), lambda h, i: (first_head + h, i, 0)),
            pl.BlockSpec((None, halo, c), lambda h, i: (first_head + h, jnp.maximum(i * hb - 1, 0), 0)),
            pl.BlockSpec((None, halo, c), lambda h, i: (first_head + h, jnp.minimum((i + 1) * hb, n_hb - 1), 0)),
            pl.BlockSpec((GDN_CONV_WIDTH, c), lambda h, i: (0, h)),
        ],
        out_specs=pl.BlockSpec((None, tr, c), lambda h, i: (h, i, 0)),
        out_shape=jax.ShapeDtypeStruct((n_heads, s, c), BF16),
        scratch_shapes=[pltpu.VMEM((tr + 2 * halo, c), F32)],
        compiler_params=_cparams(("parallel", "parallel")),
    )(heads, heads, heads, conv_w)


GDN_SEG = 32
GDN_CHAINS = 4
GDN_PREP_UNROLL = 4


def _gdn_kernel(q_ref, k_ref, v_ref, z_ref, gb_ref, prm_ref, ng_ref, o_ref,
                acc_ref, a_scr, b_scr, qp_scr, d_scr, s_scr, *, seq, chunk, seg):
    n_seg = seq // (seg * chunk)
    lane_shift = (LANES - 2 * pl.program_id(0)) % LANES
    prm = pltpu.roll(jnp.broadcast_to(prm_ref[...], (8, LANES)), lane_shift, 1)[0:1, :]
    ii = lax.broadcasted_iota(jnp.int32, (chunk, chunk), 0)
    jj = lax.broadcasted_iota(jnp.int32, (chunk, chunk), 1)
    eye = ii == jj
    row = lax.broadcasted_iota(jnp.int32, (chunk, LANES), 0)
    steps = chunk.bit_length() - 1
    acc_ref[...] = jnp.zeros_like(acc_ref)
    s_scr[...] = jnp.zeros_like(s_scr)

    def seg_of(si, d):
        return si if d == 0 else n_seg - 1 - si

    def prep_chunks(si, nls):
        nt = (((1,), (1,)), ((), ()))
        grams = []
        for nl in nls:
            for d in range(2):
                r0 = pl.multiple_of((seg_of(si, d) * seg + nl) * chunk, chunk)
                qb = q_ref[pl.ds(r0, chunk), :]
                kb = k_ref[pl.ds(r0, chunk), :]
                kk = lax.dot_general(kb, kb, nt, preferred_element_type=F32)
                qk = lax.dot_general(qb, kb, nt, preferred_element_type=F32)
                grams.append((nl, d, r0, qb, kb, kk, qk))
        chains = []
        for nl, d, r0, qb, kb, kk, qk in grams:
            qc, kc = qb.astype(F32), kb.astype(F32)
            gbt = pltpu.roll(gb_ref[pl.ds(r0, chunk), :], lane_shift, 1)
            incl = (ii >= jj) if d == 0 else (ii <= jj)
            strict = (ii > jj) if d == 0 else (ii < jj)
            for vh in range(2):
                ca, cb = 32 * d + vh, 64 + 32 * d + vh
                vc = v_ref[vh, pl.ds(r0, chunk), :].astype(F32)
                g = -jnp.exp(prm[:, ca:ca + 1]) * _softplus(gbt[:, ca:ca + 1] + prm[:, cb:cb + 1])
                beta = jnp.broadcast_to(_sigmoid(gbt[:, cb:cb + 1]), (chunk, LANES))
                gc = jnp.broadcast_to(g, (chunk, LANES))
                sh = 1
                while sh < chunk:
                    if d == 0:
                        gc = gc + jnp.where(row >= sh, pltpu.roll(gc, sh, 0), 0.0)
                    else:
                        gc = gc + jnp.where(row < chunk - sh, pltpu.roll(gc, chunk - sh, 0), 0.0)
                    sh *= 2
                tot = gc[chunk - 1:chunk, :] if d == 0 else gc[0:1, :]
                gcc = gc[:, :chunk]
                gc_row = jnp.sum(jnp.where(eye, gcc, 0.0), axis=0, keepdims=True)
                decay = jnp.exp(jnp.where(incl, gcc - gc_row, 0.0))
                eg = jnp.exp(gc)
                chains.append(dict(
                    c=2 * d + vh, vh=vh, nl=nl, r0=r0,
                    p=-jnp.where(strict, beta[:, :chunk] * kk * decay, 0.0),
                    x=jnp.concatenate([vc * beta, kc * (beta * eg)], axis=1),
                    intra=jnp.where(incl, qk * decay, 0.0).astype(BF16),
                    k_tail=(kc * jnp.exp(tot - gc)).astype(BF16),
                    q_dec=qc * eg,
                    dn=jnp.broadcast_to(jnp.exp(tot), (8, LANES))))
        for t in range(steps):
            for ch in chains:
                pb = ch["p"].astype(BF16)
                ch["x"] = ch["x"] + jnp.dot(pb, ch["x"].astype(BF16), preferred_element_type=F32)
                if t < steps - 1:
                    ch["p"] = jnp.dot(pb, pb, preferred_element_type=F32)
        for ch in chains:
            xb = ch["x"].astype(BF16)
            ch["kx"] = lax.dot_general(ch["k_tail"], xb, (((0,), (0,)), ((), ())),
                                       preferred_element_type=F32)
            ch["ix"] = jnp.dot(ch["intra"], xb, preferred_element_type=F32)
        for ch in chains:
            c, nl, kx, ix = ch["c"], ch["nl"], ch["kx"], ch["ix"]
            b_scr[c, nl] = kx[:, :HEAD_DIM]
            a_scr[c, nl] = (-kx[:, HEAD_DIM:]).astype(BF16)
            qp_scr[c, pl.ds(pl.multiple_of(nl * chunk, chunk), chunk), :] = (
                ch["q_dec"] - ix[:, HEAD_DIM:]).astype(BF16)
            d_scr[c, nl] = ch["dn"]
            acc_ref[ch["vh"], pl.ds(ch["r0"], chunk), :] += ix[:, :HEAD_DIM]


    def scan_chunk(si, i):
        work = []
        for d in range(2):
            nl = i if d == 0 else seg - 1 - i
            r0 = pl.multiple_of((seg_of(si, d) * seg + nl) * chunk, chunk)
            l0 = pl.multiple_of(nl * chunk, chunk)
            for vh in range(2):
                c = 2 * d + vh
                state = s_scr[c]
                lhs = jnp.concatenate([a_scr[c, nl], qp_scr[c, pl.ds(l0, chunk), :]], axis=0)
                r = jnp.dot(lhs, state.astype(BF16), preferred_element_type=F32)
                work.append((c, vh, nl, r0, state, r))
        for c, vh, nl, r0, state, r in work:
            acc_ref[vh, pl.ds(r0, chunk), :] += r[HEAD_DIM:]
            s_scr[c] = state * d_scr[c, nl][0:1, :] + r[:HEAD_DIM] + b_scr[c, nl]

    def segment(si, carry):
        def prep(it, cr):
            prep_chunks(si, [it * GDN_PREP_UNROLL + u for u in range(GDN_PREP_UNROLL)])
            return cr

        lax.fori_loop(0, seg // GDN_PREP_UNROLL, prep, 0)

        def scan(i, cr):
            scan_chunk(si, i)
            return cr

        lax.fori_loop(0, seg, scan, 0)
        return carry

    lax.fori_loop(0, n_seg, segment, 0)

    tile = 256

    def fin(i, carry):
        r0 = pl.multiple_of(i * tile, tile)
        for vh in range(2):
            o = acc_ref[vh, pl.ds(r0, tile), :]
            z = z_ref[vh, pl.ds(r0, tile), :].astype(F32)
            ms = jnp.mean(o * o, axis=-1, keepdims=True)
            y = o * lax.rsqrt(ms + NORM_EPS) * ng_ref[...] * (z * _sigmoid(z))
            o_ref[pl.ds(r0, tile), vh * HEAD_DIM:(vh + 1) * HEAD_DIM] = y.astype(o_ref.dtype)
        return carry

    lax.fori_loop(0, seq // tile, fin, 0)


def _gdn(gq, heads, z_first, gb, prm, norm_gain):
    _, s, c = gq.shape
    seg = GDN_SEG
    assert s % (seg * GDN_CHUNK) == 0 and z_first % 2 == 0
    once = pl.Buffered(1)
    return pl.pallas_call(
        functools.partial(_gdn_kernel, seq=s, chunk=GDN_CHUNK, seg=seg),
        grid=(GDN_QK_HEADS,),
        in_specs=[
            pl.BlockSpec((None, s, c), lambda h: (h, 0, 0), pipeline_mode=once),
            pl.BlockSpec((None, s, c), lambda h: (GDN_QK_HEADS + h, 0, 0), pipeline_mode=once),
            pl.BlockSpec((2, s, c), lambda h: (GDN_QK_HEADS + h, 0, 0), pipeline_mode=once),
            pl.BlockSpec((2, s, c), lambda h: (z_first // 2 + h, 0, 0), pipeline_mode=once),
            pl.BlockSpec((s, LANES), lambda h: (0, 0), pipeline_mode=once),
            pl.BlockSpec((1, LANES), lambda h: (0, 0)),
            pl.BlockSpec((1, c), lambda h: (0, 0)),
        ],
        out_specs=pl.BlockSpec((s, 2 * c), lambda h: (0, h)),
        out_shape=jax.ShapeDtypeStruct((s, GDN_V_HEADS * c), BF16),
        scratch_shapes=[
            pltpu.VMEM((2, s, c), F32),
            pltpu.VMEM((GDN_CHAINS, seg, c, c), BF16),
            pltpu.VMEM((GDN_CHAINS, seg, c, c), F32),
            pltpu.VMEM((GDN_CHAINS, seg * GDN_CHUNK, c), BF16),
            pltpu.VMEM((GDN_CHAINS, seg, 8, LANES), F32),
            pltpu.VMEM((GDN_CHAINS, c, c), F32),
        ],
        compiler_params=_cparams(("parallel",)),
    )(gq, gq, gq, heads, gb, prm, norm_gain)


def _branch_kernel(ya_ref, yg_ref, wa_ref, wg_ref, ga_ref, gg_ref, o_ref, wab_ref, wgb_ref):
    @pl.when(pl.program_id(1) == 0)
    def _():
        wab_ref[...] = wa_ref[...].astype(BF16)
        wgb_ref[...] = wg_ref[...].astype(BF16)

    a = jnp.dot(ya_ref[...], wab_ref[...], preferred_element_type=F32)
    g = jnp.dot(yg_ref[...], wgb_ref[...], preferred_element_type=F32)
    o_ref[...] = (ga_ref[...].astype(F32) * a + gg_ref[...].astype(F32) * g).astype(o_ref.dtype)


def _branch(y_att, y_gdn, w_a, w_g, gates, tm=1024, tn=256):
    s, da = y_att.shape
    dg = y_gdn.shape[1]
    n = w_a.shape[1]
    goff = n // tn
    return pl.pallas_call(
        _branch_kernel,
        grid=(n // tn, s // tm),
        in_specs=[
            pl.BlockSpec((tm, da), lambda j, i: (i, 0)),
            pl.BlockSpec((tm, dg), lambda j, i: (i, 0)),
            pl.BlockSpec((da, tn), lambda j, i: (0, j)),
            pl.BlockSpec((dg, tn), lambda j, i: (0, j)),
            pl.BlockSpec((tm, tn), lambda j, i: (i, j)),
            pl.BlockSpec((tm, tn), lambda j, i: (i, goff + j)),
        ],
        out_specs=pl.BlockSpec((tm, tn), lambda j, i: (i, j)),
        out_shape=jax.ShapeDtypeStruct((s, n), BF16),
        scratch_shapes=[pltpu.VMEM((da, tn), BF16), pltpu.VMEM((dg, tn), BF16)],
        compiler_params=_cparams(("parallel", "arbitrary")),
    )(y_att, y_gdn, w_a, w_g, gates, gates)


def _outproj_kernel(m_ref, w_ref, x_ref, o_ref, wb_ref):
    @pl.when(pl.program_id(1) == 0)
    def _():
        wb_ref[...] = w_ref[...].astype(BF16)

    o_ref[...] = x_ref[...] + jnp.dot(m_ref[...], wb_ref[...], preferred_element_type=F32)


def _outproj(merged, w, x, tm=1024, tn=512):
    s, d = merged.shape
    n = w.shape[1]
    return pl.pallas_call(
        _outproj_kernel,
        grid=(n // tn, s // tm),
        in_specs=[
            pl.BlockSpec((tm, d), lambda j, i: (i, 0)),
            pl.BlockSpec((d, tn), lambda j, i: (0, j)),
            pl.BlockSpec((tm, tn), lambda j, i: (i, j)),
        ],
        out_specs=pl.BlockSpec((tm, tn), lambda j, i: (i, j)),
        out_shape=jax.ShapeDtypeStruct((s, n), F32),
        scratch_shapes=[pltpu.VMEM((d, tn), BF16)],
        compiler_params=_cparams(("parallel", "arbitrary")),
    )(merged, w, x)


def _router_kernel(x_ref, g_ref, wr_ref, br_ref, h_ref, route_ref, wa_ref, wb_ref):
    x = x_ref[...]
    ms = jnp.mean(x * x, axis=-1, keepdims=True)
    h = x * lax.rsqrt(ms + NORM_EPS) * g_ref[...]
    h_ref[...] = h.astype(h_ref.dtype)
    logits = jnp.dot(h, wr_ref[...], precision=HIGHEST, preferred_element_type=F32) + br_ref[...]
    lane_i = lax.broadcasted_iota(jnp.int32, logits.shape, 1)
    lane = lane_i.astype(F32)
    neg = -1e30
    big = 1e6
    is_g = lane_i < MOE_GROUPS
    lg = jnp.where(is_g, logits, neg)
    mg = jnp.max(lg, axis=-1, keepdims=True)
    sg = jnp.sum(jnp.where(is_g, jnp.exp(lg - mg), 0.0), axis=-1, keepdims=True)
    group_w = 1.0 / sg
    gid = jnp.min(jnp.where(is_g & (lg == mg), lane, big), axis=-1, keepdims=True)
    e_lane = lane_i - MOE_GROUPS
    in_grp = (e_lane >= 0) & (e_lane < MOE_EXPERTS) & ((e_lane // MOE_EXPERTS_PER_GROUP).astype(F32) == gid)
    le = jnp.where(in_grp, logits, neg)
    m1 = jnp.max(le, axis=-1, keepdims=True)
    i1 = jnp.min(jnp.where(in_grp & (le == m1), lane, big), axis=-1, keepdims=True)
    rest = in_grp & (lane != i1)
    le2 = jnp.where(rest, logits, neg)
    m2 = jnp.max(le2, axis=-1, keepdims=True)
    i2 = jnp.min(jnp.where(rest & (le2 == m2), lane, big), axis=-1, keepdims=True)
    se = jnp.sum(jnp.where(in_grp, jnp.exp(le - m1), 0.0), axis=-1, keepdims=True)
    p1 = 1.0 / se
    p2 = jnp.exp(m2 - m1) / se
    den = p1 + p2
    w1 = group_w * (p1 / den)
    w2 = group_w * (p2 / den)
    e1 = i1 - MOE_GROUPS
    e2 = i2 - MOE_GROUPS
    route_ref[...] = jnp.where(lane_i == 0, e1, jnp.where(lane_i == 1, e2, 0.0))
    wa_ref[...] = jnp.broadcast_to(w1, wa_ref.shape)
    wb_ref[...] = jnp.broadcast_to(w2, wb_ref.shape)


def _router(x1, gain, w_router, b_router, tm=512):
    s, d = x1.shape
    row = pl.BlockSpec((tm, LANES), lambda i: (i, 0))
    return pl.pallas_call(
        _router_kernel,
        grid=(s // tm,),
        in_specs=[
            pl.BlockSpec((tm, d), lambda i: (i, 0)),
            pl.BlockSpec((1, d), lambda i: (0, 0)),
            pl.BlockSpec((d, LANES), lambda i: (0, 0)),
            pl.BlockSpec((1, LANES), lambda i: (0, 0)),
        ],
        out_specs=[pl.BlockSpec((tm, d), lambda i: (i, 0)), row, row, row],
        out_shape=[jax.ShapeDtypeStruct((s, d), BF16)] + [jax.ShapeDtypeStruct((s, LANES), F32)] * 3,
        compiler_params=_cparams(("parallel",)),
    )(x1, gain.reshape(1, d), w_router, b_router)


def _rank_kernel(route_ref, rank_ref, cnt_ref, run_ref, *, tm):
    @pl.when(pl.program_id(0) == 0)
    def _():
        run_ref[...] = jnp.zeros_like(run_ref)

    r = route_ref[...]
    lane_i = lax.broadcasted_iota(jnp.int32, r.shape, 1)
    lane = lane_i.astype(F32)
    oa = (lane == r[:, 0:1]).astype(F32)
    ob = (lane == r[:, 1:2]).astype(F32)
    both = oa + ob
    tri = (lax.broadcasted_iota(jnp.int32, (tm, tm), 0)
           > lax.broadcasted_iota(jnp.int32, (tm, tm), 1)).astype(BF16)
    before = run_ref[0:1, :] + jnp.dot(tri, both.astype(BF16), preferred_element_type=F32)
    ra = jnp.sum(oa * before, axis=-1, keepdims=True)
    rb = jnp.sum(ob * before, axis=-1, keepdims=True)
    ka = r[:, 0:1] * float(MOE_KEY_STRIDE) + ra
    kb = r[:, 1:2] * float(MOE_KEY_STRIDE) + rb
    rank_ref[...] = jnp.where(lane_i == 0, ka, jnp.where(lane_i == 1, kb, 0.0))
    run_ref[0:1, :] = run_ref[0:1, :] + jnp.sum(both, axis=0, keepdims=True)
    cnt_ref[...] = run_ref[...]


def _rank(route, tm=512):
    s = route.shape[0]
    return pl.pallas_call(
        functools.partial(_rank_kernel, tm=tm),
        grid=(s // tm,),
        in_specs=[pl.BlockSpec((tm, LANES), lambda i: (i, 0))],
        out_specs=[pl.BlockSpec((tm, LANES), lambda i: (i, 0)), pl.BlockSpec((8, LANES), lambda i: (0, 0))],
        out_shape=[jax.ShapeDtypeStruct((s, LANES), F32), jax.ShapeDtypeStruct((8, LANES), F32)],
        scratch_shapes=[pltpu.VMEM((8, LANES), F32)],
        compiler_params=_cparams(("arbitrary",)),
    )(route)


def _slot(key_ref, ps_ref, t):
    key = key_ref[t]
    return ps_ref[key >> MOE_KEY_SHIFT] + (key & (MOE_KEY_STRIDE - 1))


def _dispatch_kernel(ka_ref, kb_ref, ps_ref, h_ref, xs_in_ref, xs_ref, sem, *, tt):
    del xs_in_ref
    base = pl.program_id(0) * tt

    def copy(t, slot):
        return pltpu.make_async_copy(h_ref.at[t], xs_ref.at[slot], sem)

    def issue(t, c):
        copy(t, _slot(ka_ref, ps_ref, base + t)).start()
        copy(t, _slot(kb_ref, ps_ref, base + t)).start()
        return c

    lax.fori_loop(0, tt, issue, 0)

    def drain(t, c):
        copy(0, 0).wait()
        copy(0, 0).wait()
        return c

    lax.fori_loop(0, tt, drain, 0)


def _dispatch(h2, key_a, key_b, starts, n_slots, tt=512):
    s, d = h2.shape
    sub = d // LANES
    h3 = h2.reshape(s, sub, LANES)
    xs0 = jnp.zeros((n_slots, sub, LANES), h2.dtype)
    grid_spec = pltpu.PrefetchScalarGridSpec(
        num_scalar_prefetch=3,
        grid=(s // tt,),
        in_specs=[pl.BlockSpec((tt, sub, LANES), lambda i, ka, kb, ps: (i, 0, 0)),
                  pl.BlockSpec(memory_space=pl.ANY)],
        out_specs=pl.BlockSpec(memory_space=pl.ANY),
        scratch_shapes=[pltpu.SemaphoreType.DMA(())],
    )
    xs = pl.pallas_call(
        functools.partial(_dispatch_kernel, tt=tt),
        grid_spec=grid_spec,
        out_shape=jax.ShapeDtypeStruct(xs0.shape, xs0.dtype),
        input_output_aliases={4: 0},
        compiler_params=_cparams(("arbitrary",)),
    )(key_a, key_b, starts, h3, xs0)
    return xs.reshape(n_slots, d)


def _expert_kernel(be_ref, nu_ref, nb_ref, x_ref, wg_hbm, wu_hbm, wd_hbm, y_ref,
                   wg_buf, wu_buf, wd_buf, wgb_ref, wub_ref, wdb_ref, sem, slot_ref):
    b = pl.program_id(0)
    n_used = nu_ref[0]

    def fetch(e, slot):
        return (pltpu.make_async_copy(wg_hbm.at[e], wg_buf.at[slot], sem.at[0, slot]),
                pltpu.make_async_copy(wu_hbm.at[e], wu_buf.at[slot], sem.at[1, slot]),
                pltpu.make_async_copy(wd_hbm.at[e], wd_buf.at[slot], sem.at[2, slot]))

    @pl.when(b == 0)
    def _():
        slot_ref[0] = 0
        for cp in fetch(be_ref[0], 0):
            cp.start()

    @pl.when(b < n_used)
    def _():
        e = be_ref[b]
        first = jnp.logical_or(b == 0, e != be_ref[jnp.maximum(b - 1, 0)])

        @pl.when(first)
        def _():
            slot = slot_ref[0]
            for cp in fetch(e, slot):
                cp.wait()
            nxt = b + nb_ref[e]

            @pl.when(nxt < n_used)
            def _():
                for cp in fetch(be_ref[jnp.minimum(nxt, be_ref.shape[0] - 1)], 1 - slot):
                    cp.start()

            wgb_ref[...] = wg_buf[slot].astype(BF16)
            wub_ref[...] = wu_buf[slot].astype(BF16)
            wdb_ref[...] = wd_buf[slot].astype(BF16)
            slot_ref[0] = 1 - slot

        x = x_ref[...]
        g = jnp.dot(x, wgb_ref[...], preferred_element_type=F32)
        u = jnp.dot(x, wub_ref[...], preferred_element_type=F32)
        mid = (g * _sigmoid(g) * u).astype(BF16)
        y_ref[...] = jnp.dot(mid, wdb_ref[...], preferred_element_type=F32)

    @pl.when(b >= nu_ref[0])
    def _():
        y_ref[...] = jnp.zeros_like(y_ref)


def _experts(xs, block_expert, n_used, expert_blocks, w_gate, w_up, w_down, bm=MOE_BLOCK):
    n_slots, d = xs.shape
    ff = w_gate.shape[2]
    n_blocks = n_slots // bm
    hbm = pl.BlockSpec(memory_space=pl.ANY)
    grid_spec = pltpu.PrefetchScalarGridSpec(
        num_scalar_prefetch=3,
        grid=(n_blocks,),
        in_specs=[pl.BlockSpec((bm, d), lambda b, be, nu, nb: (jnp.minimum(b, nu[0] - 1), 0)),
                  hbm, hbm, hbm],
        out_specs=pl.BlockSpec((bm, d), lambda b, be, nu, nb: (b, 0)),
        scratch_shapes=[
            pltpu.VMEM((2, d, ff), F32), pltpu.VMEM((2, d, ff), F32), pltpu.VMEM((2, ff, d), F32),
            pltpu.VMEM((d, ff), BF16), pltpu.VMEM((d, ff), BF16), pltpu.VMEM((ff, d), BF16),
            pltpu.SemaphoreType.DMA((3, 2)),
            pltpu.SMEM((1,), jnp.int32),
        ],
    )
    return pl.pallas_call(
        _expert_kernel,
        grid_spec=grid_spec,
        out_shape=jax.ShapeDtypeStruct((n_slots, d), F32),
        compiler_params=_cparams(("arbitrary",)),
    )(block_expert, n_used, expert_blocks, xs, w_gate, w_up, w_down)


def _combine_kernel(ka_ref, kb_ref, ps_ref, x_ref, wa_ref, wb_ref, y_ref, o_ref, bufa, bufb, sem, *, tt):
    base = pl.program_id(0) * tt

    def copy_a(t, slot):
        return pltpu.make_async_copy(y_ref.at[slot], bufa.at[t], sem.at[0])

    def copy_b(t, slot):
        return pltpu.make_async_copy(y_ref.at[slot], bufb.at[t], sem.at[1])

    def issue(t, c):
        copy_a(t, _slot(ka_ref, ps_ref, base + t)).start()
        copy_b(t, _slot(kb_ref, ps_ref, base + t)).start()
        return c

    lax.fori_loop(0, tt, issue, 0)

    def drain(t, c):
        copy_a(0, 0).wait()
        copy_b(0, 0).wait()
        return c

    lax.fori_loop(0, tt, drain, 0)
    o_ref[...] = x_ref[...] + wa_ref[...] * bufa[...] + wb_ref[...] * bufb[...]


def _combine(x1, w_a, w_b, y, key_a, key_b, starts, tt=256):
    s, d = x1.shape
    sub = d // LANES
    x3 = x1.reshape(s, sub, LANES)
    y3 = y.reshape(y.shape[0], sub, LANES)
    tok = pl.BlockSpec((tt, sub, LANES), lambda i, ka, kb, ps: (i, 0, 0))
    wsp = pl.BlockSpec((tt, 1, LANES), lambda i, ka, kb, ps: (i, 0, 0))
    grid_spec = pltpu.PrefetchScalarGridSpec(
        num_scalar_prefetch=3,
        grid=(s // tt,),
        in_specs=[tok, wsp, wsp, pl.BlockSpec(memory_space=pl.ANY)],
        out_specs=tok,
        scratch_shapes=[pltpu.VMEM((tt, sub, LANES), F32), pltpu.VMEM((tt, sub, LANES), F32),
                        pltpu.SemaphoreType.DMA((2,))],
    )
    out = pl.pallas_call(
        functools.partial(_combine_kernel, tt=tt),
        grid_spec=grid_spec,
        out_shape=jax.ShapeDtypeStruct(x3.shape, F32),
        compiler_params=_cparams(("arbitrary",)),
    )(key_a, key_b, starts, x3, w_a.reshape(s, 1, LANES), w_b.reshape(s, 1, LANES), y3)
    return out.reshape(s, d)


def _moe(x1, gain, w_gr, b_gr, w_er, b_er, w_gate, w_up, w_down):
    s, d = x1.shape
    pad = LANES - MOE_GROUPS - MOE_EXPERTS
    w_router = jnp.concatenate([w_gr, w_er, jnp.zeros((d, pad), F32)], axis=1)
    b_router = jnp.concatenate([b_gr, b_er, jnp.zeros((pad,), F32)]).reshape(1, LANES)
    h2, route, w_a, w_b = _router(x1, gain, w_router, b_router)
    rank, cnt = _rank(route)
    counts = cnt[0, :MOE_EXPERTS].astype(jnp.int32)
    padded = (counts + MOE_BLOCK - 1) // MOE_BLOCK * MOE_BLOCK
    padded_ends = jnp.cumsum(padded)
    padded_starts = padded_ends - padded
    assert 2 * s <= MOE_KEY_STRIDE
    key_a = rank[:, 0].astype(jnp.int32)
    key_b = rank[:, 1].astype(jnp.int32)
    starts = padded_starts.astype(jnp.int32)
    n_blocks = -(-(2 * s) // MOE_BLOCK) + MOE_EXPERTS
    n_slots = n_blocks * MOE_BLOCK
    block_expert = jnp.clip(
        jnp.searchsorted(padded_ends, jnp.arange(n_blocks, dtype=jnp.int32) * MOE_BLOCK, side="right"),
        0, MOE_EXPERTS - 1).astype(jnp.int32)
    n_used = (padded_ends[-1:] // MOE_BLOCK).astype(jnp.int32)
    xs = _dispatch(h2, key_a, key_b, starts, n_slots)
    expert_blocks = (padded // MOE_BLOCK).astype(jnp.int32)
    y = _experts(xs, block_expert, n_used, expert_blocks, w_gate, w_up, w_down)
    return _combine(x1, w_a, w_b, y, key_a, key_b, starts)


def _alibi_slopes():
    n = ATT_GROUPS * ATT_HEADS
    s = jnp.exp2(-ALIBI_MAX_BIAS * jnp.arange(1, n + 1, dtype=F32) / n)
    return s.reshape(ATT_GROUPS, ATT_HEADS)


def _layer(x, norm1_gain, w_in, q_norm_gain, k_norm_gain, gdn_conv_w, gdn_a_log, gdn_dt_bias,
           gdn_norm_gain, w_branch_att, w_branch_gdn, w_out, norm2_gain, w_group_router,
           b_group_router, w_expert_router, b_expert_router, w_gate, w_up, w_down):
    s, d = x.shape
    h = _rmsnorm(x, norm1_gain)

    qg = jnp.broadcast_to(q_norm_gain[:, None, :] * (HEAD_DIM ** -0.5), (ATT_GROUPS, ATT_HEADS, HEAD_DIM))
    kg = jnp.broadcast_to(k_norm_gain[:, None, :], (ATT_GROUPS, ATT_HEADS, HEAD_DIM))
    qk_gain = jnp.concatenate([qg.reshape(-1), kg.reshape(-1)]).reshape(1, QK_COLS)

    tn = 1024
    dils = [dil for _, dil in ATT_PATTERNS]
    v_att = _inproj(h, w_in, lambda j: QK_COLS // tn + j, ATT_W // tn, "heads", dils=dils)
    v_att = [v.reshape(ATT_HEADS, dil, s // dil, HEAD_DIM) for v, dil in zip(v_att, dils)]
    heads = _inproj(h, w_in, lambda j: (QK_COLS + ATT_W) // tn + j, (HEADS_COLS - ATT_W) // tn, "heads")
    ab = _inproj(h, w_in, lambda j: AB_OFF // AB_COLS + j, 1, "f32", tn=AB_COLS)
    gates = _inproj(h, w_in[:, GATE_OFF:], lambda j: j, GATE_COLS // tn, "sigmoid")

    slopes = _alibi_slopes()
    outs, lses = [], []
    for g, (window, dil) in enumerate(ATT_PATTERNS):
        blk = lambda j, g=g: 2 * g + j + jnp.where(j >= 2, ATT_GROUPS * ATT_W // tn - 2, 0)
        qk = _inproj(h, w_in, blk, 4, "qknorm", gain=qk_gain, gain_block=blk, dils=(dil,))
        qk = qk.reshape(2 * ATT_HEADS, dil, s // dil, HEAD_DIM)
        o, lse = _attn_group(qk, v_att[g], slopes[g], window, dil)
        outs.append(o)
        lses.append(lse)
    y_att = _attn_combine(outs, lses, dils)

    gq = _gdn_conv(heads, gdn_conv_w, 0)
    prm = jnp.concatenate([gdn_a_log.reshape(-1), gdn_dt_bias.reshape(-1)]).reshape(1, LANES)
    z_first = 2 * GDN_QK_HEADS + GDN_V_HEADS
    y_gdn = _gdn(gq, heads, z_first, ab, prm, gdn_norm_gain.reshape(1, HEAD_DIM))

    merged = _branch(y_att, y_gdn, w_branch_att, w_branch_gdn, gates)
    x1 = _outproj(merged, w_out, x)
    return _moe(x1, norm2_gain, w_group_router, b_group_router, w_expert_router, b_expert_router,
                w_gate, w_up, w_down)


def kernel(x, norm1_gain, w_in, q_norm_gain, k_norm_gain, gdn_conv_w, gdn_a_log, gdn_dt_bias,
           gdn_norm_gain, w_branch_att, w_branch_gdn, w_out, norm2_gain, w_group_router,
           b_group_router, w_expert_router, b_expert_router, w_gate, w_up, w_down):
    b, s, d = x.shape
    params = (norm1_gain, w_in, q_norm_gain, k_norm_gain, gdn_conv_w, gdn_a_log, gdn_dt_bias,
              gdn_norm_gain, w_branch_att, w_branch_gdn, w_out, norm2_gain, w_group_router,
              b_group_router, w_expert_router, b_expert_router, w_gate, w_up, w_down)
    outs = []
    for bi in range(b):
        xb = x[bi]
        for i in range(norm1_gain.shape[0]):
            xb = _layer(xb, *(p[i] for p in params))
        outs.append(xb)
    return jnp.stack(outs, axis=0)
```

```python
import functools
import math

import jax
import jax.numpy as jnp
from jax import lax
from jax.experimental import pallas as pl
from jax.experimental.pallas import tpu as pltpu

F32 = jnp.float32
BF16 = jnp.bfloat16
HIGHEST = lax.Precision.HIGHEST

NORM_EPS = 1e-6
D_MODEL = 2048
HEAD_DIM = 128
ATT_HEADS = 16
ATT_PATTERNS = ((128, 1), (512, 4), (2048, 16))
ATT_GROUPS = len(ATT_PATTERNS)
ALIBI_MAX_BIAS = 8.0
GDN_QK_HEADS = 16
GDN_V_HEADS = 32
GDN_CONV_WIDTH = 5
GDN_CHUNK = 64
MOE_GROUPS = 8
MOE_EXPERTS_PER_GROUP = 8
MOE_EXPERTS = 64
MOE_FF = 512
MOE_BLOCK = 128
MOE_KEY_SHIFT = 15
MOE_KEY_STRIDE = 1 << MOE_KEY_SHIFT
TOKEN_WORD_ROWS = D_MODEL // (2 * 128)
TOKEN_F32_ROWS = D_MODEL // 128

ATT_W = ATT_HEADS * HEAD_DIM
QK_COLS = 2 * ATT_GROUPS * ATT_W
HEADS_COLS = ATT_W + 2 * GDN_QK_HEADS * HEAD_DIM + 2 * GDN_V_HEADS * HEAD_DIM
AB_OFF = QK_COLS + HEADS_COLS
AB_COLS = 4 * GDN_V_HEADS
GATE_OFF = AB_OFF + AB_COLS
GATE_COLS = 2 * D_MODEL

VMEM_LIMIT = 56 * 1024 * 1024
LANES = 128


def _cparams(sem, vmem=VMEM_LIMIT):
    return pltpu.CompilerParams(dimension_semantics=sem, vmem_limit_bytes=vmem)


def _sigmoid(x):
    return 1.0 / (1.0 + jnp.exp(-x))


def _softplus(x):
    return jnp.maximum(x, 0.0) + jnp.log(1.0 + jnp.exp(-jnp.abs(x)))


def _rmsnorm_kernel(x_ref, g_ref, o_ref):
    x = x_ref[...]
    ms = jnp.mean(x * x, axis=-1, keepdims=True)
    o_ref[...] = (x * lax.rsqrt(ms + NORM_EPS) * g_ref[...]).astype(o_ref.dtype)


def _rmsnorm(x, gain, tm=512):
    s, d = x.shape
    return pl.pallas_call(
        _rmsnorm_kernel,
        grid=(s // tm,),
        in_specs=[pl.BlockSpec((tm, d), lambda i: (i, 0)), pl.BlockSpec((1, d), lambda i: (0, 0))],
        out_specs=pl.BlockSpec((tm, d), lambda i: (i, 0)),
        out_shape=jax.ShapeDtypeStruct((s, d), BF16),
        compiler_params=_cparams(("parallel",)),
    )(x, gain.reshape(1, d))


def _inproj_kernel(*refs, mode, tm, tn, dils):
    refs = list(refs)
    x_ref, w_ref = refs[:2]
    pos = 2
    g_ref = None
    if mode == "qknorm":
        g_ref = refs[pos]
        pos += 1
    n_out = len(dils) if mode in ("qknorm", "heads") else 1
    o_refs = refs[pos:pos + n_out]
    wb_ref = refs[pos + n_out]
    dl_ref = refs[pos + n_out + 1] if max(dils) > 1 else None

    @pl.when(pl.program_id(1) == 0)
    def _():
        wb_ref[...] = w_ref[...].astype(BF16)

    acc = jnp.dot(x_ref[...], wb_ref[...], preferred_element_type=F32)
    if mode in ("qknorm", "heads"):
        for c in range(tn // HEAD_DIM):
            a = acc[:, c * HEAD_DIM:(c + 1) * HEAD_DIM]
            if mode == "qknorm":
                ms = jnp.mean(a * a, axis=-1, keepdims=True)
                a = a * lax.rsqrt(ms + NORM_EPS) * g_ref[:, c * HEAD_DIM:(c + 1) * HEAD_DIM]
            if dl_ref is not None:
                dl_ref[...] = a
            for o_ref, dil in zip(o_refs, dils):
                if dil == 1:
                    o_ref[c] = a.astype(o_ref.dtype)
                else:
                    for r in range(dil):
                        o_ref[c, r] = dl_ref[pl.ds(r, tm // dil, stride=dil), :].astype(o_ref.dtype)
    elif mode == "f32":
        o_refs[0][...] = acc
    else:
        o_refs[0][...] = _sigmoid(acc).astype(o_refs[0].dtype)


def _inproj(h, w, col_block, n_tiles, mode, gain=None, gain_block=None, dils=(1,), tm=1024, tn=1024):
    s, d = h.shape
    assert s % tm == 0
    in_specs = [pl.BlockSpec((tm, d), lambda j, i: (i, 0)),
                pl.BlockSpec((d, tn), lambda j, i: (0, col_block(j)))]
    args = [h, w]
    if mode == "qknorm":
        in_specs.append(pl.BlockSpec((1, tn), lambda j, i: (0, gain_block(j))))
        args.append(gain)
    scratch = [pltpu.VMEM((d, tn), BF16)]
    if mode in ("qknorm", "heads"):
        hpt = tn // HEAD_DIM
        n_heads = n_tiles * hpt
        out_specs, out_shape = [], []
        for dil in dils:
            if dil == 1:
                out_specs.append(pl.BlockSpec((hpt, tm, HEAD_DIM), lambda j, i: (j, i, 0)))
                out_shape.append(jax.ShapeDtypeStruct((n_heads, s, HEAD_DIM), BF16))
            else:
                assert tm % (16 * dil) == 0
                out_specs.append(pl.BlockSpec((hpt, dil, tm // dil, HEAD_DIM), lambda j, i: (j, 0, i, 0)))
                out_shape.append(jax.ShapeDtypeStruct((n_heads, dil, s // dil, HEAD_DIM), BF16))
        if max(dils) > 1:
            scratch.append(pltpu.VMEM((tm, HEAD_DIM), F32))
    else:
        out_specs = [pl.BlockSpec((tm, tn), lambda j, i: (i, j))]
        out_shape = [jax.ShapeDtypeStruct((s, n_tiles * tn), F32 if mode == "f32" else BF16)]
    outs = pl.pallas_call(
        functools.partial(_inproj_kernel, mode=mode, tm=tm, tn=tn, dils=tuple(dils)),
        grid=(n_tiles, s // tm),
        in_specs=in_specs,
        out_specs=out_specs,
        out_shape=out_shape,
        scratch_shapes=scratch,
        compiler_params=_cparams(("parallel", "arbitrary")),
    )(*args)
    return outs if len(outs) > 1 else outs[0]


ATT_TILES_PER_STEP = 4


def _attn_kernel(slope_ref, q_ref, k_ref, v_ref, o_ref, lse_ref, *, length, tq, half):
    slope = slope_ref[pl.program_id(0)]
    win = tq + 2 * half
    n_t = length // tq
    per_step = math.gcd(n_t, ATT_TILES_PER_STEP)
    base = (lax.broadcasted_iota(jnp.int32, (tq, win), 0)
            - lax.broadcasted_iota(jnp.int32, (tq, win), 1))
    eye = (lax.broadcasted_iota(jnp.int32, (tq, tq), 0)
           == lax.broadcasted_iota(jnp.int32, (tq, tq), 1))

    def bias_for(offset):
        dist = jnp.abs(base + offset)
        return jnp.where(dist <= half, -slope * dist.astype(F32), -1e30)

    bias_first, bias_mid, bias_last = bias_for(0), bias_for(half), bias_for(2 * half)

    def body(it, carry):
        tiles = []
        for u in range(per_step):
            i = it * per_step + u
            q0 = pl.multiple_of(i * tq, tq)
            start = pl.multiple_of(jnp.clip(q0 - half, 0, length - win), half)
            q = q_ref[pl.ds(q0, tq), :]
            k = k_ref[pl.ds(start, win), :]
            s = lax.dot_general(q, k, (((1,), (1,)), ((), ())), preferred_element_type=F32)
            tiles.append((i, q0, start, s))
        soft = []
        for i, q0, start, s in tiles:
            bias = jnp.where(i == 0, bias_first, jnp.where(i == n_t - 1, bias_last, bias_mid))
            s = s + bias
            m = jnp.max(s, axis=-1, keepdims=True)
            p = jnp.exp(s - m)
            l = jnp.sum(p, axis=-1, keepdims=True)
            soft.append((m, l, p.astype(BF16)))
        pvs = [jnp.dot(p, v_ref[pl.ds(start, win), :], preferred_element_type=F32)
               for (_, _, start, _), (_, _, p) in zip(tiles, soft)]
        for (i, q0, _, _), (m, l, _), pv in zip(tiles, soft, pvs):
            o_ref[pl.ds(q0, tq), :] = (pv / l).astype(o_ref.dtype)
            lse = m + jnp.log(l)
            lse_ref[pl.ds(i, 1), :] = jnp.sum(jnp.where(eye, lse, 0.0), axis=0, keepdims=True)
        return carry

    lax.fori_loop(0, n_t // per_step, body, 0)


def _attn_group(qk, v, slopes, window, dil, tq=128):
    _, _, length, c = qk.shape
    half = window // (2 * dil)
    assert length >= 2 * tq and length >= tq + 2 * half and length % tq == 0 and tq >= half
    n_t = length // tq
    sub = lambda off: pl.BlockSpec((None, None, length, c), lambda h, r, sl: (off + h, r, 0, 0))
    grid_spec = pltpu.PrefetchScalarGridSpec(
        num_scalar_prefetch=1,
        grid=(ATT_HEADS, dil),
        in_specs=[sub(0), sub(ATT_HEADS), sub(0)],
        out_specs=[sub(0), pl.BlockSpec((None, None, n_t, tq), lambda h, r, sl: (h, r, 0, 0))],
    )
    o, lse = pl.pallas_call(
        functools.partial(_attn_kernel, length=length, tq=tq, half=half),
        grid_spec=grid_spec,
        out_shape=[jax.ShapeDtypeStruct((ATT_HEADS, dil, length, c), BF16),
                   jax.ShapeDtypeStruct((ATT_HEADS, dil, n_t, tq), F32)],
        compiler_params=_cparams(("parallel", "parallel")),
    )(slopes * float(dil), qk, qk, v)
    lse = lse.reshape(ATT_HEADS, dil, length).transpose(2, 1, 0).reshape(length * dil, ATT_HEADS)
    return o, lse


def _attn_combine_kernel(o1_ref, o2_ref, o3_ref, l1_ref, l2_ref, l3_ref, y_ref, il_ref, *, ts, dils):
    l1, l2, l3 = l1_ref[...], l2_ref[...], l3_ref[...]
    m = jnp.maximum(jnp.maximum(l1, l2), l3)
    e1, e2, e3 = jnp.exp(l1 - m), jnp.exp(l2 - m), jnp.exp(l3 - m)
    den = e1 + e2 + e3
    ws = (e1 / den, e2 / den, e3 / den)
    for h in range(ATT_HEADS):
        y = jnp.zeros((ts, HEAD_DIM), F32)
        for o_ref, w, dil in zip((o1_ref, o2_ref, o3_ref), ws, dils):
            if dil == 1:
                og = o_ref[h, 0].astype(F32)
            else:
                for r in range(dil):
                    il_ref[pl.ds(r, ts // dil, stride=dil), :] = o_ref[h, r].astype(F32)
                og = il_ref[...]
            y = y + w[:, h:h + 1] * og
        y_ref[:, h * HEAD_DIM:(h + 1) * HEAD_DIM] = y.astype(y_ref.dtype)


def _attn_combine(outs, lses, dils, ts=512):
    s = lses[0].shape[0]
    c = HEAD_DIM
    ospecs = [pl.BlockSpec((ATT_HEADS, dil, ts // dil, c), lambda i: (0, 0, i, 0)) for dil in dils]
    lspec = pl.BlockSpec((ts, ATT_HEADS), lambda i: (i, 0))
    return pl.pallas_call(
        functools.partial(_attn_combine_kernel, ts=ts, dils=tuple(dils)),
        grid=(s // ts,),
        in_specs=ospecs + [lspec] * 3,
        out_specs=pl.BlockSpec((ts, ATT_W), lambda i: (i, 0)),
        out_shape=jax.ShapeDtypeStruct((s, ATT_W), BF16),
        scratch_shapes=[pltpu.VMEM((ts, c), F32)],
        compiler_params=_cparams(("parallel",)),
    )(*outs, *lses)


def _gdn_conv_kernel(x_ref, p_ref, n_ref, w_ref, o_ref, xc_ref, *, tr, halo):
    c = pl.program_id(0)
    i = pl.program_id(1)
    x = x_ref[...].astype(F32)
    prev = jnp.where(i > 0, p_ref[...].astype(F32), 0.0)
    nxt = jnp.where(i < pl.num_programs(1) - 1, n_ref[...].astype(F32), 0.0)
    xc_ref[0:halo, :] = prev
    xc_ref[halo:halo + tr, :] = x
    xc_ref[halo + tr:, :] = nxt
    w = w_ref[...]
    y = jnp.zeros((tr, HEAD_DIM), F32)
    for j in range(GDN_CONV_WIDTH):
        off = halo + j - (GDN_CONV_WIDTH - 1) // 2
        y = y + xc_ref[off:off + tr, :] * w[j:j + 1]
    y = y * _sigmoid(y)
    nrm = lax.rsqrt(jnp.sum(y * y, axis=-1, keepdims=True) + NORM_EPS)
    scale = jnp.where(c < GDN_QK_HEADS, nrm * (HEAD_DIM ** -0.5),
                      jnp.where(c < 2 * GDN_QK_HEADS, nrm, 1.0))
    o_ref[...] = (y * scale).astype(o_ref.dtype)


def _gdn_conv(heads, conv_w, first_head, tr=2048, halo=16):
    _, s, c = heads.shape
    n_heads = 2 * GDN_QK_HEADS + GDN_V_HEADS
    hb = tr // halo
    n_hb = s // halo
    return pl.pallas_call(
        functools.partial(_gdn_conv_kernel, tr=tr, halo=halo),
        grid=(n_heads, s // tr),
        in_specs=[
            pl.BlockSpec((None, tr, c), lambda h, i: (first_head + h, i, 0)),
            pl.BlockSpec((None, halo, c), lambda h, i: (first_head + h, jnp.maximum(i * hb - 1, 0), 0)),
            pl.BlockSpec((None, halo, c), lambda h, i: (first_head + h, jnp.minimum((i + 1) * hb, n_hb - 1), 0)),
            pl.BlockSpec((GDN_CONV_WIDTH, c), lambda h, i: (0, h)),
        ],
        out_specs=pl.BlockSpec((None, tr, c), lambda h, i: (h, i, 0)),
        out_shape=jax.ShapeDtypeStruct((n_heads, s, c), BF16),
        scratch_shapes=[pltpu.VMEM((tr + 2 * halo, c), F32)],
        compiler_params=_cparams(("parallel", "parallel")),
    )(heads, heads, heads, conv_w)


GDN_SEG = 32
GDN_CHAINS = 4
GDN_PREP_UNROLL = 4


def _gdn_kernel(q_ref, k_ref, v_ref, z_ref, gb_ref, prm_ref, ng_ref, o_ref,
                acc_ref, a_scr, b_scr, qp_scr, d_scr, s_scr, *, seq, chunk, seg):
    n_seg = seq // (seg * chunk)
    lane_shift = (LANES - 2 * pl.program_id(0)) % LANES
    prm = pltpu.roll(jnp.broadcast_to(prm_ref[...], (8, LANES)), lane_shift, 1)[0:1, :]
    ii = lax.broadcasted_iota(jnp.int32, (chunk, chunk), 0)
    jj = lax.broadcasted_iota(jnp.int32, (chunk, chunk), 1)
    eye = ii == jj
    row = lax.broadcasted_iota(jnp.int32, (chunk, LANES), 0)
    steps = chunk.bit_length() - 1
    acc_ref[...] = jnp.zeros_like(acc_ref)
    s_scr[...] = jnp.zeros_like(s_scr)

    def seg_of(si, d):
        return si if d == 0 else n_seg - 1 - si

    def prep_chunks(si, nls):
        nt = (((1,), (1,)), ((), ()))
        grams = []
        for nl in nls:
            for d in range(2):
                r0 = pl.multiple_of((seg_of(si, d) * seg + nl) * chunk, chunk)
                qb = q_ref[pl.ds(r0, chunk), :]
                kb = k_ref[pl.ds(r0, chunk), :]
                kk = lax.dot_general(kb, kb, nt, preferred_element_type=F32)
                qk = lax.dot_general(qb, kb, nt, preferred_element_type=F32)
                grams.append((nl, d, r0, qb, kb, kk, qk))
        chains = []
        for nl, d, r0, qb, kb, kk, qk in grams:
            qc, kc = qb.astype(F32), kb.astype(F32)
            gbt = pltpu.roll(gb_ref[pl.ds(r0, chunk), :], lane_shift, 1)
            incl = (ii >= jj) if d == 0 else (ii <= jj)
            strict = (ii > jj) if d == 0 else (ii < jj)
            for vh in range(2):
                ca, cb = 32 * d + vh, 64 + 32 * d + vh
                vc = v_ref[vh, pl.ds(r0, chunk), :].astype(F32)
                g = -jnp.exp(prm[:, ca:ca + 1]) * _softplus(gbt[:, ca:ca + 1] + prm[:, cb:cb + 1])
                beta = jnp.broadcast_to(_sigmoid(gbt[:, cb:cb + 1]), (chunk, LANES))
                gc = jnp.broadcast_to(g, (chunk, LANES))
                sh = 1
                while sh < chunk:
                    if d == 0:
                        gc = gc + jnp.where(row >= sh, pltpu.roll(gc, sh, 0), 0.0)
                    else:
                        gc = gc + jnp.where(row < chunk - sh, pltpu.roll(gc, chunk - sh, 0), 0.0)
                    sh *= 2
                tot = gc[chunk - 1:chunk, :] if d == 0 else gc[0:1, :]
                gcc = gc[:, :chunk]
                gc_row = jnp.sum(jnp.where(eye, gcc, 0.0), axis=0, keepdims=True)
                decay = jnp.exp(jnp.where(incl, gcc - gc_row, 0.0))
                eg = jnp.exp(gc)
                chains.append(dict(
                    c=2 * d + vh, vh=vh, nl=nl, r0=r0,
                    p=-jnp.where(strict, beta[:, :chunk] * kk * decay, 0.0),
                    x=jnp.concatenate([vc * beta, kc * (beta * eg)], axis=1),
                    intra=jnp.where(incl, qk * decay, 0.0).astype(BF16),
                    k_tail=(kc * jnp.exp(tot - gc)).astype(BF16),
                    q_dec=qc * eg,
                    dn=jnp.broadcast_to(jnp.exp(tot), (8, LANES))))
        for t in range(steps):
            for ch in chains:
                pb = ch["p"].astype(BF16)
                ch["x"] = ch["x"] + jnp.dot(pb, ch["x"].astype(BF16), preferred_element_type=F32)
                if t < steps - 1:
                    ch["p"] = jnp.dot(pb, pb, preferred_element_type=F32)
        for ch in chains:
            xb = ch["x"].astype(BF16)
            ch["kx"] = lax.dot_general(ch["k_tail"], xb, (((0,), (0,)), ((), ())),
                                       preferred_element_type=F32)
            ch["ix"] = jnp.dot(ch["intra"], xb, preferred_element_type=F32)
        for ch in chains:
            c, nl, kx, ix = ch["c"], ch["nl"], ch["kx"], ch["ix"]
            b_scr[c, nl] = kx[:, :HEAD_DIM]
            a_scr[c, nl] = (-kx[:, HEAD_DIM:]).astype(BF16)
            qp_scr[c, pl.ds(pl.multiple_of(nl * chunk, chunk), chunk), :] = (
                ch["q_dec"] - ix[:, HEAD_DIM:]).astype(BF16)
            d_scr[c, nl] = ch["dn"]
            acc_ref[ch["vh"], pl.ds(ch["r0"], chunk), :] += ix[:, :HEAD_DIM]


    def scan_chunk(si, i):
        work = []
        for d in range(2):
            nl = i if d == 0 else seg - 1 - i
            r0 = pl.multiple_of((seg_of(si, d) * seg + nl) * chunk, chunk)
            l0 = pl.multiple_of(nl * chunk, chunk)
            for vh in range(2):
                c = 2 * d + vh
                state = s_scr[c]
                lhs = jnp.concatenate([a_scr[c, nl], qp_scr[c, pl.ds(l0, chunk), :]], axis=0)
                r = jnp.dot(lhs, state.astype(BF16), preferred_element_type=F32)
                work.append((c, vh, nl, r0, state, r))
        for c, vh, nl, r0, state, r in work:
            acc_ref[vh, pl.ds(r0, chunk), :] += r[HEAD_DIM:]
            s_scr[c] = state * d_scr[c, nl][0:1, :] + r[:HEAD_DIM] + b_scr[c, nl]

    def segment(si, carry):
        def prep(it, cr):
            prep_chunks(si, [it * GDN_PREP_UNROLL + u for u in range(GDN_PREP_UNROLL)])
            return cr

        lax.fori_loop(0, seg // GDN_PREP_UNROLL, prep, 0)

        def scan(i, cr):
            scan_chunk(si, i)
            return cr

        lax.fori_loop(0, seg, scan, 0)
        return carry

    lax.fori_loop(0, n_seg, segment, 0)

    tile = 256

    def fin(i, carry):
        r0 = pl.multiple_of(i * tile, tile)
        for vh in range(2):
            o = acc_ref[vh, pl.ds(r0, tile), :]
            z = z_ref[vh, pl.ds(r0, tile), :].astype(F32)
            ms = jnp.mean(o * o, axis=-1, keepdims=True)
            y = o * lax.rsqrt(ms + NORM_EPS) * ng_ref[...] * (z * _sigmoid(z))
            o_ref[pl.ds(r0, tile), vh * HEAD_DIM:(vh + 1) * HEAD_DIM] = y.astype(o_ref.dtype)
        return carry

    lax.fori_loop(0, seq // tile, fin, 0)


def _gdn(gq, heads, z_first, gb, prm, norm_gain):
    _, s, c = gq.shape
    seg = GDN_SEG
    assert s % (seg * GDN_CHUNK) == 0 and z_first % 2 == 0
    once = pl.Buffered(1)
    return pl.pallas_call(
        functools.partial(_gdn_kernel, seq=s, chunk=GDN_CHUNK, seg=seg),
        grid=(GDN_QK_HEADS,),
        in_specs=[
            pl.BlockSpec((None, s, c), lambda h: (h, 0, 0), pipeline_mode=once),
            pl.BlockSpec((None, s, c), lambda h: (GDN_QK_HEADS + h, 0, 0), pipeline_mode=once),
            pl.BlockSpec((2, s, c), lambda h: (GDN_QK_HEADS + h, 0, 0), pipeline_mode=once),
            pl.BlockSpec((2, s, c), lambda h: (z_first // 2 + h, 0, 0), pipeline_mode=once),
            pl.BlockSpec((s, LANES), lambda h: (0, 0), pipeline_mode=once),
            pl.BlockSpec((1, LANES), lambda h: (0, 0)),
            pl.BlockSpec((1, c), lambda h: (0, 0)),
        ],
        out_specs=pl.BlockSpec((s, 2 * c), lambda h: (0, h)),
        out_shape=jax.ShapeDtypeStruct((s, GDN_V_HEADS * c), BF16),
        scratch_shapes=[
            pltpu.VMEM((2, s, c), F32),
            pltpu.VMEM((GDN_CHAINS, seg, c, c), BF16),
            pltpu.VMEM((GDN_CHAINS, seg, c, c), F32),
            pltpu.VMEM((GDN_CHAINS, seg * GDN_CHUNK, c), BF16),
            pltpu.VMEM((GDN_CHAINS, seg, 8, LANES), F32),
            pltpu.VMEM((GDN_CHAINS, c, c), F32),
        ],
        compiler_params=_cparams(("parallel",)),
    )(gq, gq, gq, heads, gb, prm, norm_gain)


def _branch_kernel(ya_ref, yg_ref, wa_ref, wg_ref, ga_ref, gg_ref, o_ref, wab_ref, wgb_ref):
    @pl.when(pl.program_id(1) == 0)
    def _():
        wab_ref[...] = wa_ref[...].astype(BF16)
        wgb_ref[...] = wg_ref[...].astype(BF16)

    a = jnp.dot(ya_ref[...], wab_ref[...], preferred_element_type=F32)
    g = jnp.dot(yg_ref[...], wgb_ref[...], preferred_element_type=F32)
    o_ref[...] = (ga_ref[...].astype(F32) * a + gg_ref[...].astype(F32) * g).astype(o_ref.dtype)


def _branch(y_att, y_gdn, w_a, w_g, gates, tm=1024, tn=512):
    s, da = y_att.shape
    dg = y_gdn.shape[1]
    n = w_a.shape[1]
    goff = n // tn
    once = pl.Buffered(1)
    return pl.pallas_call(
        _branch_kernel,
        grid=(n // tn, s // tm),
        in_specs=[
            pl.BlockSpec((tm, da), lambda j, i: (i, 0)),
            pl.BlockSpec((tm, dg), lambda j, i: (i, 0)),
            pl.BlockSpec((da, tn), lambda j, i: (0, j), pipeline_mode=once),
            pl.BlockSpec((dg, tn), lambda j, i: (0, j), pipeline_mode=once),
            pl.BlockSpec((tm, tn), lambda j, i: (i, j)),
            pl.BlockSpec((tm, tn), lambda j, i: (i, goff + j)),
        ],
        out_specs=pl.BlockSpec((tm, tn), lambda j, i: (i, j)),
        out_shape=jax.ShapeDtypeStruct((s, n), BF16),
        scratch_shapes=[pltpu.VMEM((da, tn), BF16), pltpu.VMEM((dg, tn), BF16)],
        compiler_params=_cparams(("parallel", "arbitrary")),
    )(y_att, y_gdn, w_a, w_g, gates, gates)


def _outproj_kernel(m_ref, w_ref, x_ref, o_ref, wb_ref):
    @pl.when(pl.program_id(1) == 0)
    def _():
        wb_ref[...] = w_ref[...].astype(BF16)

    o_ref[...] = x_ref[...] + jnp.dot(m_ref[...], wb_ref[...], preferred_element_type=F32)


def _outproj(merged, w, x, tm=1024, tn=512):
    s, d = merged.shape
    n = w.shape[1]
    return pl.pallas_call(
        _outproj_kernel,
        grid=(n // tn, s // tm),
        in_specs=[
            pl.BlockSpec((tm, d), lambda j, i: (i, 0)),
            pl.BlockSpec((d, tn), lambda j, i: (0, j)),
            pl.BlockSpec((tm, tn), lambda j, i: (i, j)),
        ],
        out_specs=pl.BlockSpec((tm, tn), lambda j, i: (i, j)),
        out_shape=jax.ShapeDtypeStruct((s, n), F32),
        scratch_shapes=[pltpu.VMEM((d, tn), BF16)],
        compiler_params=_cparams(("parallel", "arbitrary")),
    )(merged, w, x)


def _router_kernel(x_ref, g_ref, wr_ref, br_ref, h_ref, route_ref, wa_ref, wb_ref):
    x = x_ref[...]
    ms = jnp.mean(x * x, axis=-1, keepdims=True)
    h = x * lax.rsqrt(ms + NORM_EPS) * g_ref[...]
    tm = x.shape[0]
    hb = lax.bitcast_convert_type(h.astype(BF16).astype(F32), jnp.uint32)
    for i in range(x.shape[1] // (2 * LANES)):
        lo = hb[:, (2 * i) * LANES:(2 * i + 1) * LANES]
        hi = hb[:, (2 * i + 1) * LANES:(2 * i + 2) * LANES]
        h_ref[pl.ds(i, tm, stride=TOKEN_WORD_ROWS), :] = hi | (lo >> 16)
    logits = jnp.dot(h, wr_ref[...], precision=HIGHEST, preferred_element_type=F32) + br_ref[...]
    lane_i = lax.broadcasted_iota(jnp.int32, logits.shape, 1)
    lane = lane_i.astype(F32)
    neg = -1e30
    big = 1e6
    is_g = lane_i < MOE_GROUPS
    lg = jnp.where(is_g, logits, neg)
    mg = jnp.max(lg, axis=-1, keepdims=True)
    sg = jnp.sum(jnp.where(is_g, jnp.exp(lg - mg), 0.0), axis=-1, keepdims=True)
    group_w = 1.0 / sg
    gid = jnp.min(jnp.where(is_g & (lg == mg), lane, big), axis=-1, keepdims=True)
    e_lane = lane_i - MOE_GROUPS
    in_grp = (e_lane >= 0) & (e_lane < MOE_EXPERTS) & ((e_lane // MOE_EXPERTS_PER_GROUP).astype(F32) == gid)
    le = jnp.where(in_grp, logits, neg)
    m1 = jnp.max(le, axis=-1, keepdims=True)
    i1 = jnp.min(jnp.where(in_grp & (le == m1), lane, big), axis=-1, keepdims=True)
    rest = in_grp & (lane != i1)
    le2 = jnp.where(rest, logits, neg)
    m2 = jnp.max(le2, axis=-1, keepdims=True)
    i2 = jnp.min(jnp.where(rest & (le2 == m2), lane, big), axis=-1, keepdims=True)
    se = jnp.sum(jnp.where(in_grp, jnp.exp(le - m1), 0.0), axis=-1, keepdims=True)
    p1 = 1.0 / se
    p2 = jnp.exp(m2 - m1) / se
    den = p1 + p2
    w1 = group_w * (p1 / den)
    w2 = group_w * (p2 / den)
    e1 = i1 - MOE_GROUPS
    e2 = i2 - MOE_GROUPS
    route_ref[...] = jnp.where(lane_i == 0, e1, jnp.where(lane_i == 1, e2, 0.0))
    wa_ref[...] = jnp.broadcast_to(w1, wa_ref.shape)
    wb_ref[...] = jnp.broadcast_to(w2, wb_ref.shape)


def _router(x1, gain, w_router, b_router, tm=512):
    s, d = x1.shape
    row = pl.BlockSpec((tm, LANES), lambda i: (i, 0))
    return pl.pallas_call(
        _router_kernel,
        grid=(s // tm,),
        in_specs=[
            pl.BlockSpec((tm, d), lambda i: (i, 0)),
            pl.BlockSpec((1, d), lambda i: (0, 0)),
            pl.BlockSpec((d, LANES), lambda i: (0, 0)),
            pl.BlockSpec((1, LANES), lambda i: (0, 0)),
        ],
        out_specs=[pl.BlockSpec((tm * TOKEN_WORD_ROWS, LANES), lambda i: (i, 0)), row, row, row],
        out_shape=([jax.ShapeDtypeStruct((s * TOKEN_WORD_ROWS, LANES), jnp.uint32)]
                   + [jax.ShapeDtypeStruct((s, LANES), F32)] * 3),
        compiler_params=_cparams(("parallel",)),
    )(x1, gain.reshape(1, d), w_router, b_router)


def _rank_kernel(route_ref, rank_ref, cnt_ref, run_ref, *, tm):
    @pl.when(pl.program_id(0) == 0)
    def _():
        run_ref[...] = jnp.zeros_like(run_ref)

    r = route_ref[...]
    lane_i = lax.broadcasted_iota(jnp.int32, r.shape, 1)
    lane = lane_i.astype(F32)
    oa = (lane == r[:, 0:1]).astype(F32)
    ob = (lane == r[:, 1:2]).astype(F32)
    both = oa + ob
    tri = (lax.broadcasted_iota(jnp.int32, (tm, tm), 0)
           > lax.broadcasted_iota(jnp.int32, (tm, tm), 1)).astype(BF16)
    before = run_ref[0:1, :] + jnp.dot(tri, both.astype(BF16), preferred_element_type=F32)
    ra = jnp.sum(oa * before, axis=-1, keepdims=True)
    rb = jnp.sum(ob * before, axis=-1, keepdims=True)
    ka = r[:, 0:1] * float(MOE_KEY_STRIDE) + ra
    kb = r[:, 1:2] * float(MOE_KEY_STRIDE) + rb
    rank_ref[...] = jnp.where(lane_i == 0, ka, jnp.where(lane_i == 1, kb, 0.0))
    run_ref[0:1, :] = run_ref[0:1, :] + jnp.sum(both, axis=0, keepdims=True)
    cnt_ref[...] = run_ref[...]


def _rank(route, tm=512):
    s = route.shape[0]
    return pl.pallas_call(
        functools.partial(_rank_kernel, tm=tm),
        grid=(s // tm,),
        in_specs=[pl.BlockSpec((tm, LANES), lambda i: (i, 0))],
        out_specs=[pl.BlockSpec((tm, LANES), lambda i: (i, 0)), pl.BlockSpec((8, LANES), lambda i: (0, 0))],
        out_shape=[jax.ShapeDtypeStruct((s, LANES), F32), jax.ShapeDtypeStruct((8, LANES), F32)],
        scratch_shapes=[pltpu.VMEM((8, LANES), F32)],
        compiler_params=_cparams(("arbitrary",)),
    )(route)


def _slot(key_ref, ps_ref, t):
    key = key_ref[t]
    return ps_ref[key >> MOE_KEY_SHIFT] + (key & (MOE_KEY_STRIDE - 1))


def _dispatch_kernel(ka_ref, kb_ref, ps_ref, h_ref, xs_in_ref, xs_ref, sem, *, tt):
    del xs_in_ref
    base = pl.program_id(0) * tt

    rows = TOKEN_WORD_ROWS

    def copy(t, slot):
        return pltpu.make_async_copy(h_ref.at[pl.ds(pl.multiple_of(t * rows, rows), rows)],
                                     xs_ref.at[pl.ds(pl.multiple_of(slot * rows, rows), rows)], sem)

    def issue(t, c):
        copy(t, _slot(ka_ref, ps_ref, base + t)).start()
        copy(t, _slot(kb_ref, ps_ref, base + t)).start()
        return c

    lax.fori_loop(0, tt, issue, 0)

    def drain(t, c):
        copy(0, 0).wait()
        copy(0, 0).wait()
        return c

    lax.fori_loop(0, tt, drain, 0)


def _dispatch(hp, key_a, key_b, starts, n_slots, tt=512):
    rows = TOKEN_WORD_ROWS
    s = hp.shape[0] // rows
    xs0 = jnp.zeros((n_slots * rows, LANES), hp.dtype)
    grid_spec = pltpu.PrefetchScalarGridSpec(
        num_scalar_prefetch=3,
        grid=(s // tt,),
        in_specs=[pl.BlockSpec((tt * rows, LANES), lambda i, ka, kb, ps: (i, 0)),
                  pl.BlockSpec(memory_space=pl.ANY)],
        out_specs=pl.BlockSpec(memory_space=pl.ANY),
        scratch_shapes=[pltpu.SemaphoreType.DMA(())],
    )
    xs = pl.pallas_call(
        functools.partial(_dispatch_kernel, tt=tt),
        grid_spec=grid_spec,
        out_shape=jax.ShapeDtypeStruct(xs0.shape, xs0.dtype),
        input_output_aliases={4: 0},
        compiler_params=_cparams(("arbitrary",)),
    )(key_a, key_b, starts, hp, xs0)
    return xs


def _expert_kernel(be_ref, nu_ref, nb_ref, x_ref, wg_hbm, wu_hbm, wd_hbm, y_ref,
                   wg_buf, wu_buf, wd_buf, wgb_ref, wub_ref, wdb_ref, sem, slot_ref, *, bm):
    b = pl.program_id(0)
    n_used = nu_ref[0]

    def fetch(e, slot):
        return (pltpu.make_async_copy(wg_hbm.at[e], wg_buf.at[slot], sem.at[0, slot]),
                pltpu.make_async_copy(wu_hbm.at[e], wu_buf.at[slot], sem.at[1, slot]),
                pltpu.make_async_copy(wd_hbm.at[e], wd_buf.at[slot], sem.at[2, slot]))

    @pl.when(b == 0)
    def _():
        slot_ref[0] = 0
        for cp in fetch(be_ref[0], 0):
            cp.start()

    @pl.when(b < n_used)
    def _():
        e = be_ref[b]
        first = jnp.logical_or(b == 0, e != be_ref[jnp.maximum(b - 1, 0)])

        @pl.when(first)
        def _():
            slot = slot_ref[0]
            for cp in fetch(e, slot):
                cp.wait()
            nxt = b + nb_ref[e]

            @pl.when(nxt < n_used)
            def _():
                for cp in fetch(be_ref[jnp.minimum(nxt, be_ref.shape[0] - 1)], 1 - slot):
                    cp.start()

            wgb_ref[...] = wg_buf[slot].astype(BF16)
            wub_ref[...] = wu_buf[slot].astype(BF16)
            wdb_ref[...] = wd_buf[slot].astype(BF16)
            slot_ref[0] = 1 - slot

        parts = []
        for i in range(TOKEN_WORD_ROWS):
            word = x_ref[pl.ds(i, bm, stride=TOKEN_WORD_ROWS), :]
            parts.append(lax.bitcast_convert_type(word << 16, F32).astype(BF16))
            parts.append(lax.bitcast_convert_type(word & jnp.uint32(0xFFFF0000), F32).astype(BF16))
        x = jnp.concatenate(parts, axis=1)
        g = jnp.dot(x, wgb_ref[...], preferred_element_type=F32)
        u = jnp.dot(x, wub_ref[...], preferred_element_type=F32)
        mid = (g * _sigmoid(g) * u).astype(BF16)
        y = jnp.dot(mid, wdb_ref[...], preferred_element_type=F32)
        for c in range(TOKEN_F32_ROWS):
            y_ref[pl.ds(c, bm, stride=TOKEN_F32_ROWS), :] = y[:, c * LANES:(c + 1) * LANES]

    @pl.when(b >= nu_ref[0])
    def _():
        y_ref[...] = jnp.zeros_like(y_ref)


def _experts(xs, block_expert, n_used, expert_blocks, w_gate, w_up, w_down, bm=MOE_BLOCK):
    n_slots = xs.shape[0] // TOKEN_WORD_ROWS
    d, ff = w_gate.shape[1], w_gate.shape[2]
    n_blocks = n_slots // bm
    hbm = pl.BlockSpec(memory_space=pl.ANY)
    grid_spec = pltpu.PrefetchScalarGridSpec(
        num_scalar_prefetch=3,
        grid=(n_blocks,),
        in_specs=[pl.BlockSpec((bm * TOKEN_WORD_ROWS, LANES),
                               lambda b, be, nu, nb: (jnp.minimum(b, nu[0] - 1), 0)),
                  hbm, hbm, hbm],
        out_specs=pl.BlockSpec((bm * TOKEN_F32_ROWS, LANES), lambda b, be, nu, nb: (b, 0)),
        scratch_shapes=[
            pltpu.VMEM((2, d, ff), F32), pltpu.VMEM((2, d, ff), F32), pltpu.VMEM((2, ff, d), F32),
            pltpu.VMEM((d, ff), BF16), pltpu.VMEM((d, ff), BF16), pltpu.VMEM((ff, d), BF16),
            pltpu.SemaphoreType.DMA((3, 2)),
            pltpu.SMEM((1,), jnp.int32),
        ],
    )
    return pl.pallas_call(
        functools.partial(_expert_kernel, bm=bm),
        grid_spec=grid_spec,
        out_shape=jax.ShapeDtypeStruct((n_slots * TOKEN_F32_ROWS, LANES), F32),
        compiler_params=_cparams(("arbitrary",)),
    )(block_expert, n_used, expert_blocks, xs, w_gate, w_up, w_down)


def _combine_kernel(ka_ref, kb_ref, ps_ref, x_ref, wa_ref, wb_ref, y_ref, o_ref, bufa, bufb, sem, *, tt):
    base = pl.program_id(0) * tt

    rows = TOKEN_F32_ROWS

    def slab(ref, i):
        return ref.at[pl.ds(pl.multiple_of(i * rows, rows), rows)]

    def copy_a(t, slot):
        return pltpu.make_async_copy(slab(y_ref, slot), slab(bufa, t), sem.at[0])

    def copy_b(t, slot):
        return pltpu.make_async_copy(slab(y_ref, slot), slab(bufb, t), sem.at[1])

    def issue(t, c):
        copy_a(t, _slot(ka_ref, ps_ref, base + t)).start()
        copy_b(t, _slot(kb_ref, ps_ref, base + t)).start()
        return c

    lax.fori_loop(0, tt, issue, 0)

    def drain(t, c):
        copy_a(0, 0).wait()
        copy_b(0, 0).wait()
        return c

    lax.fori_loop(0, tt, drain, 0)
    wa, wb = wa_ref[...], wb_ref[...]
    for c in range(rows):
        ya = bufa[pl.ds(c, tt, stride=rows), :]
        yb = bufb[pl.ds(c, tt, stride=rows), :]
        cols = slice(c * LANES, (c + 1) * LANES)
        o_ref[:, cols] = x_ref[:, cols] + wa * ya + wb * yb


def _combine(x1, w_a, w_b, y, key_a, key_b, starts, tt=256):
    s, d = x1.shape
    rows = TOKEN_F32_ROWS
    tok = pl.BlockSpec((tt, d), lambda i, ka, kb, ps: (i, 0))
    wsp = pl.BlockSpec((tt, LANES), lambda i, ka, kb, ps: (i, 0))
    grid_spec = pltpu.PrefetchScalarGridSpec(
        num_scalar_prefetch=3,
        grid=(s // tt,),
        in_specs=[tok, wsp, wsp, pl.BlockSpec(memory_space=pl.ANY)],
        out_specs=tok,
        scratch_shapes=[pltpu.VMEM((tt * rows, LANES), F32), pltpu.VMEM((tt * rows, LANES), F32),
                        pltpu.SemaphoreType.DMA((2,))],
    )
    return pl.pallas_call(
        functools.partial(_combine_kernel, tt=tt),
        grid_spec=grid_spec,
        out_shape=jax.ShapeDtypeStruct((s, d), F32),
        compiler_params=_cparams(("arbitrary",)),
    )(key_a, key_b, starts, x1, w_a, w_b, y)


def _moe(x1, gain, w_gr, b_gr, w_er, b_er, w_gate, w_up, w_down):
    s, d = x1.shape
    pad = LANES - MOE_GROUPS - MOE_EXPERTS
    w_router = jnp.concatenate([w_gr, w_er, jnp.zeros((d, pad), F32)], axis=1)
    b_router = jnp.concatenate([b_gr, b_er, jnp.zeros((pad,), F32)]).reshape(1, LANES)
    h2, route, w_a, w_b = _router(x1, gain, w_router, b_router)
    rank, cnt = _rank(route)
    counts = cnt[0, :MOE_EXPERTS].astype(jnp.int32)
    padded = (counts + MOE_BLOCK - 1) // MOE_BLOCK * MOE_BLOCK
    padded_ends = jnp.cumsum(padded)
    padded_starts = padded_ends - padded
    assert 2 * s <= MOE_KEY_STRIDE
    key_a = rank[:, 0].astype(jnp.int32)
    key_b = rank[:, 1].astype(jnp.int32)
    starts = padded_starts.astype(jnp.int32)
    n_blocks = -(-(2 * s) // MOE_BLOCK) + MOE_EXPERTS
    n_slots = n_blocks * MOE_BLOCK
    block_expert = jnp.clip(
        jnp.searchsorted(padded_ends, jnp.arange(n_blocks, dtype=jnp.int32) * MOE_BLOCK, side="right"),
        0, MOE_EXPERTS - 1).astype(jnp.int32)
    n_used = (padded_ends[-1:] // MOE_BLOCK).astype(jnp.int32)
    xs = _dispatch(h2, key_a, key_b, starts, n_slots)
    expert_blocks = (padded // MOE_BLOCK).astype(jnp.int32)
    y = _experts(xs, block_expert, n_used, expert_blocks, w_gate, w_up, w_down)
    return _combine(x1, w_a, w_b, y, key_a, key_b, starts)


def _alibi_slopes():
    n = ATT_GROUPS * ATT_HEADS
    s = jnp.exp2(-ALIBI_MAX_BIAS * jnp.arange(1, n + 1, dtype=F32) / n)
    return s.reshape(ATT_GROUPS, ATT_HEADS)


def _layer(x, norm1_gain, w_in, q_norm_gain, k_norm_gain, gdn_conv_w, gdn_a_log, gdn_dt_bias,
           gdn_norm_gain, w_branch_att, w_branch_gdn, w_out, norm2_gain, w_group_router,
           b_group_router, w_expert_router, b_expert_router, w_gate, w_up, w_down):
    s, d = x.shape
    h = _rmsnorm(x, norm1_gain)

    qg = jnp.broadcast_to(q_norm_gain[:, None, :] * (HEAD_DIM ** -0.5), (ATT_GROUPS, ATT_HEADS, HEAD_DIM))
    kg = jnp.broadcast_to(k_norm_gain[:, None, :], (ATT_GROUPS, ATT_HEADS, HEAD_DIM))
    qk_gain = jnp.concatenate([qg.reshape(-1), kg.reshape(-1)]).reshape(1, QK_COLS)

    tn = 1024
    dils = [dil for _, dil in ATT_PATTERNS]
    v_att = _inproj(h, w_in, lambda j: QK_COLS // tn + j, ATT_W // tn, "heads", dils=dils)
    v_att = [v.reshape(ATT_HEADS, dil, s // dil, HEAD_DIM) for v, dil in zip(v_att, dils)]
    heads = _inproj(h, w_in, lambda j: (QK_COLS + ATT_W) // tn + j, (HEADS_COLS - ATT_W) // tn, "heads")
    ab = _inproj(h, w_in, lambda j: AB_OFF // AB_COLS + j, 1, "f32", tn=AB_COLS)
    gates = _inproj(h, w_in[:, GATE_OFF:], lambda j: j, GATE_COLS // tn, "sigmoid")

    slopes = _alibi_slopes()
    outs, lses = [], []
    for g, (window, dil) in enumerate(ATT_PATTERNS):
        blk = lambda j, g=g: 2 * g + j + jnp.where(j >= 2, ATT_GROUPS * ATT_W // tn - 2, 0)
        qk = _inproj(h, w_in, blk, 4, "qknorm", gain=qk_gain, gain_block=blk, dils=(dil,))
        qk = qk.reshape(2 * ATT_HEADS, dil, s // dil, HEAD_DIM)
        o, lse = _attn_group(qk, v_att[g], slopes[g], window, dil)
        outs.append(o)
        lses.append(lse)
    y_att = _attn_combine(outs, lses, dils)

    gq = _gdn_conv(heads, gdn_conv_w, 0)
    prm = jnp.concatenate([gdn_a_log.reshape(-1), gdn_dt_bias.reshape(-1)]).reshape(1, LANES)
    z_first = 2 * GDN_QK_HEADS + GDN_V_HEADS
    y_gdn = _gdn(gq, heads, z_first, ab, prm, gdn_norm_gain.reshape(1, HEAD_DIM))

    merged = _branch(y_att, y_gdn, w_branch_att, w_branch_gdn, gates)
    x1 = _outproj(merged, w_out, x)
    return _moe(x1, norm2_gain, w_group_router, b_group_router, w_expert_router, b_expert_router,
                w_gate, w_up, w_down)


def kernel(x, norm1_gain, w_in, q_norm_gain, k_norm_gain, gdn_conv_w, gdn_a_log, gdn_dt_bias,
           gdn_norm_gain, w_branch_att, w_branch_gdn, w_out, norm2_gain, w_group_router,
           b_group_router, w_expert_router, b_expert_router, w_gate, w_up, w_down):
    b, s, d = x.shape
    params = (norm1_gain, w_in, q_norm_gain, k_norm_gain, gdn_conv_w, gdn_a_log, gdn_dt_bias,
              gdn_norm_gain, w_branch_att, w_branch_gdn, w_out, norm2_gain, w_group_router,
              b_group_router, w_expert_router, b_expert_router, w_gate, w_up, w_down)
    outs = []
    for bi in range(b):
        xb = x[bi]
        for i in range(norm1_gain.shape[0]):
            xb = _layer(xb, *(p[i] for p in params))
        outs.append(xb)
    return jnp.stack(outs, axis=0)
```

```python
import functools
import math

import jax
import jax.numpy as jnp
from jax import lax
from jax.experimental import pallas as pl
from jax.experimental.pallas import tpu as pltpu

F32 = jnp.float32
BF16 = jnp.bfloat16
HIGHEST = lax.Precision.HIGHEST

NORM_EPS = 1e-6
D_MODEL = 2048
HEAD_DIM = 128
ATT_HEADS = 16
ATT_PATTERNS = ((128, 1), (512, 4), (2048, 16))
ATT_GROUPS = len(ATT_PATTERNS)
ALIBI_MAX_BIAS = 8.0
GDN_QK_HEADS = 16
GDN_V_HEADS = 32
GDN_CONV_WIDTH = 5
GDN_CHUNK = 64
MOE_GROUPS = 8
MOE_EXPERTS_PER_GROUP = 8
MOE_EXPERTS = 64
MOE_FF = 512
MOE_BLOCK = 128
MOE_KEY_SHIFT = 15
MOE_KEY_STRIDE = 1 << MOE_KEY_SHIFT
TOKEN_WORD_ROWS = D_MODEL // (2 * 128)
TOKEN_F32_ROWS = D_MODEL // 128

ATT_W = ATT_HEADS * HEAD_DIM
QK_COLS = 2 * ATT_GROUPS * ATT_W
HEADS_COLS = ATT_W + 2 * GDN_QK_HEADS * HEAD_DIM + 2 * GDN_V_HEADS * HEAD_DIM
AB_OFF = QK_COLS + HEADS_COLS
AB_COLS = 4 * GDN_V_HEADS
GATE_OFF = AB_OFF + AB_COLS
GATE_COLS = 2 * D_MODEL

VMEM_LIMIT = 56 * 1024 * 1024
LANES = 128


def _cparams(sem, vmem=VMEM_LIMIT):
    return pltpu.CompilerParams(dimension_semantics=sem, vmem_limit_bytes=vmem)


def _sigmoid(x):
    return 1.0 / (1.0 + jnp.exp(-x))


def _softplus(x):
    return jnp.maximum(x, 0.0) + jnp.log(1.0 + jnp.exp(-jnp.abs(x)))


def _rmsnorm_kernel(x_ref, g_ref, o_ref):
    x = x_ref[...]
    ms = jnp.mean(x * x, axis=-1, keepdims=True)
    o_ref[...] = (x * lax.rsqrt(ms + NORM_EPS) * g_ref[...]).astype(o_ref.dtype)


def _rmsnorm(x, gain, tm=512):
    s, d = x.shape
    return pl.pallas_call(
        _rmsnorm_kernel,
        grid=(s // tm,),
        in_specs=[pl.BlockSpec((tm, d), lambda i: (i, 0)), pl.BlockSpec((1, d), lambda i: (0, 0))],
        out_specs=pl.BlockSpec((tm, d), lambda i: (i, 0)),
        out_shape=jax.ShapeDtypeStruct((s, d), BF16),
        compiler_params=_cparams(("parallel",)),
    )(x, gain.reshape(1, d))


def _inproj_kernel(*refs, mode, tm, tn, dils):
    refs = list(refs)
    x_ref, w_ref = refs[:2]
    pos = 2
    g_ref = None
    if mode == "qknorm":
        g_ref = refs[pos]
        pos += 1
    n_out = len(dils) if mode in ("qknorm", "heads") else 1
    o_refs = refs[pos:pos + n_out]
    wb_ref = refs[pos + n_out]
    dl_ref = refs[pos + n_out + 1] if max(dils) > 1 else None

    @pl.when(pl.program_id(1) == 0)
    def _():
        wb_ref[...] = w_ref[...].astype(BF16)

    acc = jnp.dot(x_ref[...], wb_ref[...], preferred_element_type=F32)
    if mode in ("qknorm", "heads"):
        for c in range(tn // HEAD_DIM):
            a = acc[:, c * HEAD_DIM:(c + 1) * HEAD_DIM]
            if mode == "qknorm":
                ms = jnp.mean(a * a, axis=-1, keepdims=True)
                a = a * lax.rsqrt(ms + NORM_EPS) * g_ref[:, c * HEAD_DIM:(c + 1) * HEAD_DIM]
            if dl_ref is not None:
                dl_ref[...] = a
            for o_ref, dil in zip(o_refs, dils):
                if dil == 1:
                    o_ref[c] = a.astype(o_ref.dtype)
                else:
                    for r in range(dil):
                        o_ref[c, r] = dl_ref[pl.ds(r, tm // dil, stride=dil), :].astype(o_ref.dtype)
    elif mode == "f32":
        o_refs[0][...] = acc
    else:
        o_refs[0][...] = _sigmoid(acc).astype(o_refs[0].dtype)


def _inproj(h, w, col_block, n_tiles, mode, gain=None, gain_block=None, dils=(1,), tm=1024, tn=1024):
    s, d = h.shape
    assert s % tm == 0
    in_specs = [pl.BlockSpec((tm, d), lambda j, i: (i, 0)),
                pl.BlockSpec((d, tn), lambda j, i: (0, col_block(j)))]
    args = [h, w]
    if mode == "qknorm":
        in_specs.append(pl.BlockSpec((1, tn), lambda j, i: (0, gain_block(j))))
        args.append(gain)
    scratch = [pltpu.VMEM((d, tn), BF16)]
    if mode in ("qknorm", "heads"):
        hpt = tn // HEAD_DIM
        n_heads = n_tiles * hpt
        out_specs, out_shape = [], []
        for dil in dils:
            if dil == 1:
                out_specs.append(pl.BlockSpec((hpt, tm, HEAD_DIM), lambda j, i: (j, i, 0)))
                out_shape.append(jax.ShapeDtypeStruct((n_heads, s, HEAD_DIM), BF16))
            else:
                assert tm % (16 * dil) == 0
                out_specs.append(pl.BlockSpec((hpt, dil, tm // dil, HEAD_DIM), lambda j, i: (j, 0, i, 0)))
                out_shape.append(jax.ShapeDtypeStruct((n_heads, dil, s // dil, HEAD_DIM), BF16))
        if max(dils) > 1:
            scratch.append(pltpu.VMEM((tm, HEAD_DIM), F32))
    else:
        out_specs = [pl.BlockSpec((tm, tn), lambda j, i: (i, j))]
        out_shape = [jax.ShapeDtypeStruct((s, n_tiles * tn), F32 if mode == "f32" else BF16)]
    outs = pl.pallas_call(
        functools.partial(_inproj_kernel, mode=mode, tm=tm, tn=tn, dils=tuple(dils)),
        grid=(n_tiles, s // tm),
        in_specs=in_specs,
        out_specs=out_specs,
        out_shape=out_shape,
        scratch_shapes=scratch,
        compiler_params=_cparams(("parallel", "arbitrary")),
    )(*args)
    return outs if len(outs) > 1 else outs[0]


ATT_TILES_PER_STEP = 4


def _attn_kernel(slope_ref, q_ref, k_ref, v_ref, o_ref, lse_ref, *, length, tq, half):
    slope = slope_ref[pl.program_id(0)]
    win = tq + 2 * half
    n_t = length // tq
    per_step = math.gcd(n_t, ATT_TILES_PER_STEP)
    base = (lax.broadcasted_iota(jnp.int32, (tq, win), 0)
            - lax.broadcasted_iota(jnp.int32, (tq, win), 1))
    eye = (lax.broadcasted_iota(jnp.int32, (tq, tq), 0)
           == lax.broadcasted_iota(jnp.int32, (tq, tq), 1))

    def bias_for(offset):
        dist = jnp.abs(base + offset)
        return jnp.where(dist <= half, -slope * dist.astype(F32), -1e30)

    bias_first, bias_mid, bias_last = bias_for(0), bias_for(half), bias_for(2 * half)

    def body(it, carry):
        tiles = []
        for u in range(per_step):
            i = it * per_step + u
            q0 = pl.multiple_of(i * tq, tq)
            start = pl.multiple_of(jnp.clip(q0 - half, 0, length - win), half)
            q = q_ref[pl.ds(q0, tq), :]
            k = k_ref[pl.ds(start, win), :]
            s = lax.dot_general(q, k, (((1,), (1,)), ((), ())), preferred_element_type=F32)
            tiles.append((i, q0, start, s))
        soft = []
        for i, q0, start, s in tiles:
            bias = jnp.where(i == 0, bias_first, jnp.where(i == n_t - 1, bias_last, bias_mid))
            s = s + bias
            m = jnp.max(s, axis=-1, keepdims=True)
            p = jnp.exp(s - m)
            l = jnp.sum(p, axis=-1, keepdims=True)
            soft.append((m, l, p.astype(BF16)))
        pvs = [jnp.dot(p, v_ref[pl.ds(start, win), :], preferred_element_type=F32)
               for (_, _, start, _), (_, _, p) in zip(tiles, soft)]
        for (i, q0, _, _), (m, l, _), pv in zip(tiles, soft, pvs):
            o_ref[pl.ds(q0, tq), :] = (pv / l).astype(o_ref.dtype)
            lse = m + jnp.log(l)
            lse_ref[pl.ds(i, 1), :] = jnp.sum(jnp.where(eye, lse, 0.0), axis=0, keepdims=True)
        return carry

    lax.fori_loop(0, n_t // per_step, body, 0)


def _attn_group(qk, v, slopes, window, dil, tq=128):
    _, _, length, c = qk.shape
    half = window // (2 * dil)
    assert length >= 2 * tq and length >= tq + 2 * half and length % tq == 0 and tq >= half
    n_t = length // tq
    sub = lambda off: pl.BlockSpec((None, None, length, c), lambda h, r, sl: (off + h, r, 0, 0))
    grid_spec = pltpu.PrefetchScalarGridSpec(
        num_scalar_prefetch=1,
        grid=(ATT_HEADS, dil),
        in_specs=[sub(0), sub(ATT_HEADS), sub(0)],
        out_specs=[sub(0), pl.BlockSpec((None, None, n_t, tq), lambda h, r, sl: (h, r, 0, 0))],
    )
    o, lse = pl.pallas_call(
        functools.partial(_attn_kernel, length=length, tq=tq, half=half),
        grid_spec=grid_spec,
        out_shape=[jax.ShapeDtypeStruct((ATT_HEADS, dil, length, c), BF16),
                   jax.ShapeDtypeStruct((ATT_HEADS, dil, n_t, tq), F32)],
        compiler_params=_cparams(("parallel", "parallel")),
    )(slopes * float(dil), qk, qk, v)
    lse = lse.reshape(ATT_HEADS, dil, length).transpose(2, 1, 0).reshape(length * dil, ATT_HEADS)
    return o, lse


def _attn_combine_kernel(o1_ref, o2_ref, o3_ref, l1_ref, l2_ref, l3_ref, y_ref, il_ref, *, ts, dils):
    l1, l2, l3 = l1_ref[...], l2_ref[...], l3_ref[...]
    m = jnp.maximum(jnp.maximum(l1, l2), l3)
    e1, e2, e3 = jnp.exp(l1 - m), jnp.exp(l2 - m), jnp.exp(l3 - m)
    den = e1 + e2 + e3
    ws = (e1 / den, e2 / den, e3 / den)
    for h in range(ATT_HEADS):
        y = jnp.zeros((ts, HEAD_DIM), F32)
        for o_ref, w, dil in zip((o1_ref, o2_ref, o3_ref), ws, dils):
            if dil == 1:
                og = o_ref[h, 0].astype(F32)
            else:
                for r in range(dil):
                    il_ref[pl.ds(r, ts // dil, stride=dil), :] = o_ref[h, r].astype(F32)
                og = il_ref[...]
            y = y + w[:, h:h + 1] * og
        y_ref[:, h * HEAD_DIM:(h + 1) * HEAD_DIM] = y.astype(y_ref.dtype)


def _attn_combine(outs, lses, dils, ts=512):
    s = lses[0].shape[0]
    c = HEAD_DIM
    ospecs = [pl.BlockSpec((ATT_HEADS, dil, ts // dil, c), lambda i: (0, 0, i, 0)) for dil in dils]
    lspec = pl.BlockSpec((ts, ATT_HEADS), lambda i: (i, 0))
    return pl.pallas_call(
        functools.partial(_attn_combine_kernel, ts=ts, dils=tuple(dils)),
        grid=(s // ts,),
        in_specs=ospecs + [lspec] * 3,
        out_specs=pl.BlockSpec((ts, ATT_W), lambda i: (i, 0)),
        out_shape=jax.ShapeDtypeStruct((s, ATT_W), BF16),
        scratch_shapes=[pltpu.VMEM((ts, c), F32)],
        compiler_params=_cparams(("parallel",)),
    )(*outs, *lses)


def _gdn_conv_kernel(x_ref, p_ref, n_ref, w_ref, o_ref, xc_ref, *, tr, halo):
    c = pl.program_id(0)
    i = pl.program_id(1)
    x = x_ref[...].astype(F32)
    prev = jnp.where(i > 0, p_ref[...].astype(F32), 0.0)
    nxt = jnp.where(i < pl.num_programs(1) - 1, n_ref[...].astype(F32), 0.0)
    xc_ref[0:halo, :] = prev
    xc_ref[halo:halo + tr, :] = x
    xc_ref[halo + tr:, :] = nxt
    w = w_ref[...]
    y = jnp.zeros((tr, HEAD_DIM), F32)
    for j in range(GDN_CONV_WIDTH):
        off = halo + j - (GDN_CONV_WIDTH - 1) // 2
        y = y + xc_ref[off:off + tr, :] * w[j:j + 1]
    y = y * _sigmoid(y)
    nrm = lax.rsqrt(jnp.sum(y * y, axis=-1, keepdims=True) + NORM_EPS)
    scale = jnp.where(c < GDN_QK_HEADS, nrm * (HEAD_DIM ** -0.5),
                      jnp.where(c < 2 * GDN_QK_HEADS, nrm, 1.0))
    o_ref[...] = (y * scale).astype(o_ref.dtype)


def _gdn_conv(heads, conv_w, first_head, tr=2048, halo=16):
    _, s, c = heads.shape
    n_heads = 2 * GDN_QK_HEADS + GDN_V_HEADS
    hb = tr // halo
    n_hb = s // halo
    return pl.pallas_call(
        functools.partial(_gdn_conv_kernel, tr=tr, halo=halo),
        grid=(n_heads, s // tr),
        in_specs=[
            pl.BlockSpec((None, tr, c), lambda h, i: (first_head + h, i, 0)),
            pl.BlockSpec((None, halo, c), lambda h, i: (first_head + h, jnp.maximum(i * hb - 1, 0), 0)),
            pl.BlockSpec((None, halo, c), lambda h, i: (first_head + h, jnp.minimum((i + 1) * hb, n_hb - 1), 0)),
            pl.BlockSpec((GDN_CONV_WIDTH, c), lambda h, i: (0, h)),
        ],
        out_specs=pl.BlockSpec((None, tr, c), lambda h, i: (h, i, 0)),
        out_shape=jax.ShapeDtypeStruct((n_heads, s, c), BF16),
        scratch_shapes=[pltpu.VMEM((tr + 2 * halo, c), F32)],
        compiler_params=_cparams(("parallel", "parallel")),
    )(heads, heads, heads, conv_w)


GDN_SEG = 32
GDN_CHAINS = 4
GDN_PREP_UNROLL = 4


def _gdn_kernel(q_ref, k_ref, v_ref, z_ref, gb_ref, prm_ref, ng_ref, o_ref,
                acc_ref, a_scr, b_scr, qp_scr, d_scr, s_scr, *, seq, chunk, seg):
    n_seg = seq // (seg * chunk)
    lane_shift = (LANES - 2 * pl.program_id(0)) % LANES
    prm = pltpu.roll(jnp.broadcast_to(prm_ref[...], (8, LANES)), lane_shift, 1)[0:1, :]
    ii = lax.broadcasted_iota(jnp.int32, (chunk, chunk), 0)
    jj = lax.broadcasted_iota(jnp.int32, (chunk, chunk), 1)
    eye = ii == jj
    row = lax.broadcasted_iota(jnp.int32, (chunk, LANES), 0)
    steps = chunk.bit_length() - 1
    acc_ref[...] = jnp.zeros_like(acc_ref)
    s_scr[...] = jnp.zeros_like(s_scr)

    def seg_of(si, d):
        return si if d == 0 else n_seg - 1 - si

    def prep_chunks(si, nls):
        nt = (((1,), (1,)), ((), ()))
        grams = []
        for nl in nls:
            for d in range(2):
                r0 = pl.multiple_of((seg_of(si, d) * seg + nl) * chunk, chunk)
                qb = q_ref[pl.ds(r0, chunk), :]
                kb = k_ref[pl.ds(r0, chunk), :]
                kk = lax.dot_general(kb, kb, nt, preferred_element_type=F32)
                qk = lax.dot_general(qb, kb, nt, preferred_element_type=F32)
                grams.append((nl, d, r0, qb, kb, kk, qk))
        chains = []
        for nl, d, r0, qb, kb, kk, qk in grams:
            qc, kc = qb.astype(F32), kb.astype(F32)
            gbt = pltpu.roll(gb_ref[pl.ds(r0, chunk), :], lane_shift, 1)
            incl = (ii >= jj) if d == 0 else (ii <= jj)
            strict = (ii > jj) if d == 0 else (ii < jj)
            for vh in range(2):
                ca, cb = 32 * d + vh, 64 + 32 * d + vh
                vc = v_ref[vh, pl.ds(r0, chunk), :].astype(F32)
                g = -jnp.exp(prm[:, ca:ca + 1]) * _softplus(gbt[:, ca:ca + 1] + prm[:, cb:cb + 1])
                beta = jnp.broadcast_to(_sigmoid(gbt[:, cb:cb + 1]), (chunk, LANES))
                gc = jnp.broadcast_to(g, (chunk, LANES))
                sh = 1
                while sh < chunk:
                    if d == 0:
                        gc = gc + jnp.where(row >= sh, pltpu.roll(gc, sh, 0), 0.0)
                    else:
                        gc = gc + jnp.where(row < chunk - sh, pltpu.roll(gc, chunk - sh, 0), 0.0)
                    sh *= 2
                tot = gc[chunk - 1:chunk, :] if d == 0 else gc[0:1, :]
                gcc = gc[:, :chunk]
                gc_row = jnp.sum(jnp.where(eye, gcc, 0.0), axis=0, keepdims=True)
                decay = jnp.exp(jnp.where(incl, gcc - gc_row, 0.0))
                eg = jnp.exp(gc)
                chains.append(dict(
                    c=2 * d + vh, vh=vh, nl=nl, r0=r0,
                    p=-jnp.where(strict, beta[:, :chunk] * kk * decay, 0.0),
                    x=jnp.concatenate([vc * beta, kc * (beta * eg)],
                                      axis=1).astype(BF16),
                    intra=jnp.where(incl, qk * decay, 0.0).astype(BF16),
                    k_tail=(kc * jnp.exp(tot - gc)).astype(BF16),
                    q_dec=qc * eg,
                    dn=jnp.broadcast_to(jnp.exp(tot), (8, LANES))))
        for ch in chains:
            ch["s"] = jnp.where(eye, 1.0, 0.0) + ch["p"]
        for ch in chains:
            pb = ch["p"].astype(BF16)
            ch["p"] = jnp.dot(pb, pb, preferred_element_type=F32)
        for t in range(1, steps):
            for ch in chains:
                pb = ch["p"].astype(BF16)
                if t < steps - 1:
                    both = jnp.dot(jnp.concatenate([ch["s"].astype(BF16), pb], axis=0), pb,
                                   preferred_element_type=F32)
                    ch["s"] = ch["s"] + both[:chunk]
                    ch["p"] = both[chunk:]
                else:
                    ch["s"] = ch["s"] + jnp.dot(ch["s"].astype(BF16), pb, preferred_element_type=F32)
        for ch in chains:
            ch["x"] = jnp.dot(ch["s"].astype(BF16), ch["x"], preferred_element_type=F32)
        for ch in chains:
            xb = ch["x"].astype(BF16)
            ch["kx"] = lax.dot_general(ch["k_tail"], xb, (((0,), (0,)), ((), ())),
                                       preferred_element_type=F32)
            ch["ix"] = jnp.dot(ch["intra"], xb, preferred_element_type=F32)
        for ch in chains:
            c, nl, kx, ix = ch["c"], ch["nl"], ch["kx"], ch["ix"]
            b_scr[c, nl] = kx[:, :HEAD_DIM]
            a_scr[c, nl] = (-kx[:, HEAD_DIM:]).astype(BF16)
            qp_scr[c, pl.ds(pl.multiple_of(nl * chunk, chunk), chunk), :] = (
                ch["q_dec"] - ix[:, HEAD_DIM:]).astype(BF16)
            d_scr[c, nl] = ch["dn"]
            acc_ref[ch["vh"], pl.ds(ch["r0"], chunk), :] += ix[:, :HEAD_DIM]


    def scan_chunk(si, i):
        work = []
        for d in range(2):
            nl = i if d == 0 else seg - 1 - i
            r0 = pl.multiple_of((seg_of(si, d) * seg + nl) * chunk, chunk)
            l0 = pl.multiple_of(nl * chunk, chunk)
            for vh in range(2):
                c = 2 * d + vh
                state = s_scr[c]
                lhs = jnp.concatenate([a_scr[c, nl], qp_scr[c, pl.ds(l0, chunk), :]], axis=0)
                r = jnp.dot(lhs, state.astype(BF16), preferred_element_type=F32)
                work.append((c, vh, nl, r0, state, r))
        for c, vh, nl, r0, state, r in work:
            acc_ref[vh, pl.ds(r0, chunk), :] += r[HEAD_DIM:]
            s_scr[c] = state * d_scr[c, nl][0:1, :] + r[:HEAD_DIM] + b_scr[c, nl]

    def segment(si, carry):
        def prep(it, cr):
            prep_chunks(si, [it * GDN_PREP_UNROLL + u for u in range(GDN_PREP_UNROLL)])
            return cr

        lax.fori_loop(0, seg // GDN_PREP_UNROLL, prep, 0)

        def scan(i, cr):
            scan_chunk(si, i)
            return cr

        lax.fori_loop(0, seg, scan, 0)
        return carry

    lax.fori_loop(0, n_seg, segment, 0)

    tile = 256

    def fin(i, carry):
        r0 = pl.multiple_of(i * tile, tile)
        for vh in range(2):
            o = acc_ref[vh, pl.ds(r0, tile), :]
            z = z_ref[vh, pl.ds(r0, tile), :].astype(F32)
            ms = jnp.mean(o * o, axis=-1, keepdims=True)
            y = o * lax.rsqrt(ms + NORM_EPS) * ng_ref[...] * (z * _sigmoid(z))
            o_ref[pl.ds(r0, tile), vh * HEAD_DIM:(vh + 1) * HEAD_DIM] = y.astype(o_ref.dtype)
        return carry

    lax.fori_loop(0, seq // tile, fin, 0)


def _gdn(gq, heads, z_first, gb, prm, norm_gain):
    _, s, c = gq.shape
    seg = GDN_SEG
    assert s % (seg * GDN_CHUNK) == 0 and z_first % 2 == 0
    once = pl.Buffered(1)
    return pl.pallas_call(
        functools.partial(_gdn_kernel, seq=s, chunk=GDN_CHUNK, seg=seg),
        grid=(GDN_QK_HEADS,),
        in_specs=[
            pl.BlockSpec((None, s, c), lambda h: (h, 0, 0), pipeline_mode=once),
            pl.BlockSpec((None, s, c), lambda h: (GDN_QK_HEADS + h, 0, 0), pipeline_mode=once),
            pl.BlockSpec((2, s, c), lambda h: (GDN_QK_HEADS + h, 0, 0), pipeline_mode=once),
            pl.BlockSpec((2, s, c), lambda h: (z_first // 2 + h, 0, 0), pipeline_mode=once),
            pl.BlockSpec((s, LANES), lambda h: (0, 0), pipeline_mode=once),
            pl.BlockSpec((1, LANES), lambda h: (0, 0)),
            pl.BlockSpec((1, c), lambda h: (0, 0)),
        ],
        out_specs=pl.BlockSpec((s, 2 * c), lambda h: (0, h)),
        out_shape=jax.ShapeDtypeStruct((s, GDN_V_HEADS * c), BF16),
        scratch_shapes=[
            pltpu.VMEM((2, s, c), F32),
            pltpu.VMEM((GDN_CHAINS, seg, c, c), BF16),
            pltpu.VMEM((GDN_CHAINS, seg, c, c), F32),
            pltpu.VMEM((GDN_CHAINS, seg * GDN_CHUNK, c), BF16),
            pltpu.VMEM((GDN_CHAINS, seg, 8, LANES), F32),
            pltpu.VMEM((GDN_CHAINS, c, c), F32),
        ],
        compiler_params=_cparams(("parallel",)),
    )(gq, gq, gq, heads, gb, prm, norm_gain)


def _branch_kernel(ya_ref, yg_ref, wa_ref, wg_ref, ga_ref, gg_ref, o_ref, wab_ref, wgb_ref):
    @pl.when(pl.program_id(1) == 0)
    def _():
        wab_ref[...] = wa_ref[...].astype(BF16)
        wgb_ref[...] = wg_ref[...].astype(BF16)

    a = jnp.dot(ya_ref[...], wab_ref[...], preferred_element_type=F32)
    g = jnp.dot(yg_ref[...], wgb_ref[...], preferred_element_type=F32)
    o_ref[...] = (ga_ref[...].astype(F32) * a + gg_ref[...].astype(F32) * g).astype(o_ref.dtype)


def _branch(y_att, y_gdn, w_a, w_g, gates, tm=1024, tn=512):
    s, da = y_att.shape
    dg = y_gdn.shape[1]
    n = w_a.shape[1]
    goff = n // tn
    once = pl.Buffered(1)
    return pl.pallas_call(
        _branch_kernel,
        grid=(n // tn, s // tm),
        in_specs=[
            pl.BlockSpec((tm, da), lambda j, i: (i, 0)),
            pl.BlockSpec((tm, dg), lambda j, i: (i, 0)),
            pl.BlockSpec((da, tn), lambda j, i: (0, j), pipeline_mode=once),
            pl.BlockSpec((dg, tn), lambda j, i: (0, j), pipeline_mode=once),
            pl.BlockSpec((tm, tn), lambda j, i: (i, j)),
            pl.BlockSpec((tm, tn), lambda j, i: (i, goff + j)),
        ],
        out_specs=pl.BlockSpec((tm, tn), lambda j, i: (i, j)),
        out_shape=jax.ShapeDtypeStruct((s, n), BF16),
        scratch_shapes=[pltpu.VMEM((da, tn), BF16), pltpu.VMEM((dg, tn), BF16)],
        compiler_params=_cparams(("parallel", "arbitrary")),
    )(y_att, y_gdn, w_a, w_g, gates, gates)


def _outproj_kernel(m_ref, w_ref, x_ref, o_ref, wb_ref):
    @pl.when(pl.program_id(1) == 0)
    def _():
        wb_ref[...] = w_ref[...].astype(BF16)

    o_ref[...] = x_ref[...] + jnp.dot(m_ref[...], wb_ref[...], preferred_element_type=F32)


def _outproj(merged, w, x, tm=1024, tn=512):
    s, d = merged.shape
    n = w.shape[1]
    return pl.pallas_call(
        _outproj_kernel,
        grid=(n // tn, s // tm),
        in_specs=[
            pl.BlockSpec((tm, d), lambda j, i: (i, 0)),
            pl.BlockSpec((d, tn), lambda j, i: (0, j)),
            pl.BlockSpec((tm, tn), lambda j, i: (i, j)),
        ],
        out_specs=pl.BlockSpec((tm, tn), lambda j, i: (i, j)),
        out_shape=jax.ShapeDtypeStruct((s, n), F32),
        scratch_shapes=[pltpu.VMEM((d, tn), BF16)],
        compiler_params=_cparams(("parallel", "arbitrary")),
    )(merged, w, x)


def _router_kernel(x_ref, g_ref, wr_ref, br_ref, h_ref, route_ref, wa_ref, wb_ref):
    x = x_ref[...]
    ms = jnp.mean(x * x, axis=-1, keepdims=True)
    h = x * lax.rsqrt(ms + NORM_EPS) * g_ref[...]
    tm = x.shape[0]
    hb = lax.bitcast_convert_type(h.astype(BF16).astype(F32), jnp.uint32)
    for i in range(x.shape[1] // (2 * LANES)):
        lo = hb[:, (2 * i) * LANES:(2 * i + 1) * LANES]
        hi = hb[:, (2 * i + 1) * LANES:(2 * i + 2) * LANES]
        h_ref[pl.ds(i, tm, stride=TOKEN_WORD_ROWS), :] = hi | (lo >> 16)
    logits = jnp.dot(h, wr_ref[...], precision=HIGHEST, preferred_element_type=F32) + br_ref[...]
    lane_i = lax.broadcasted_iota(jnp.int32, logits.shape, 1)
    lane = lane_i.astype(F32)
    neg = -1e30
    big = 1e6
    is_g = lane_i < MOE_GROUPS
    lg = jnp.where(is_g, logits, neg)
    mg = jnp.max(lg, axis=-1, keepdims=True)
    sg = jnp.sum(jnp.where(is_g, jnp.exp(lg - mg), 0.0), axis=-1, keepdims=True)
    group_w = 1.0 / sg
    gid = jnp.min(jnp.where(is_g & (lg == mg), lane, big), axis=-1, keepdims=True)
    e_lane = lane_i - MOE_GROUPS
    in_grp = (e_lane >= 0) & (e_lane < MOE_EXPERTS) & ((e_lane // MOE_EXPERTS_PER_GROUP).astype(F32) == gid)
    le = jnp.where(in_grp, logits, neg)
    m1 = jnp.max(le, axis=-1, keepdims=True)
    i1 = jnp.min(jnp.where(in_grp & (le == m1), lane, big), axis=-1, keepdims=True)
    rest = in_grp & (lane != i1)
    le2 = jnp.where(rest, logits, neg)
    m2 = jnp.max(le2, axis=-1, keepdims=True)
    i2 = jnp.min(jnp.where(rest & (le2 == m2), lane, big), axis=-1, keepdims=True)
    se = jnp.sum(jnp.where(in_grp, jnp.exp(le - m1), 0.0), axis=-1, keepdims=True)
    p1 = 1.0 / se
    p2 = jnp.exp(m2 - m1) / se
    den = p1 + p2
    w1 = group_w * (p1 / den)
    w2 = group_w * (p2 / den)
    e1 = i1 - MOE_GROUPS
    e2 = i2 - MOE_GROUPS
    route_ref[...] = jnp.where(lane_i == 0, e1, jnp.where(lane_i == 1, e2, 0.0))
    wa_ref[...] = jnp.broadcast_to(w1, wa_ref.shape)
    wb_ref[...] = jnp.broadcast_to(w2, wb_ref.shape)


def _router(x1, gain, w_router, b_router, tm=512):
    s, d = x1.shape
    row = pl.BlockSpec((tm, LANES), lambda i: (i, 0))
    return pl.pallas_call(
        _router_kernel,
        grid=(s // tm,),
        in_specs=[
            pl.BlockSpec((tm, d), lambda i: (i, 0)),
            pl.BlockSpec((1, d), lambda i: (0, 0)),
            pl.BlockSpec((d, LANES), lambda i: (0, 0)),
            pl.BlockSpec((1, LANES), lambda i: (0, 0)),
        ],
        out_specs=[pl.BlockSpec((tm * TOKEN_WORD_ROWS, LANES), lambda i: (i, 0)), row, row, row],
        out_shape=([jax.ShapeDtypeStruct((s * TOKEN_WORD_ROWS, LANES), jnp.uint32)]
                   + [jax.ShapeDtypeStruct((s, LANES), F32)] * 3),
        compiler_params=_cparams(("parallel",)),
    )(x1, gain.reshape(1, d), w_router, b_router)


def _rank_kernel(route_ref, rank_ref, cnt_ref, run_ref, *, tm):
    @pl.when(pl.program_id(0) == 0)
    def _():
        run_ref[...] = jnp.zeros_like(run_ref)

    r = route_ref[...]
    lane_i = lax.broadcasted_iota(jnp.int32, r.shape, 1)
    lane = lane_i.astype(F32)
    oa = (lane == r[:, 0:1]).astype(F32)
    ob = (lane == r[:, 1:2]).astype(F32)
    both = oa + ob
    tri = (lax.broadcasted_iota(jnp.int32, (tm, tm), 0)
           > lax.broadcasted_iota(jnp.int32, (tm, tm), 1)).astype(BF16)
    before = run_ref[0:1, :] + jnp.dot(tri, both.astype(BF16), preferred_element_type=F32)
    ra = jnp.sum(oa * before, axis=-1, keepdims=True)
    rb = jnp.sum(ob * before, axis=-1, keepdims=True)
    ka = r[:, 0:1] * float(MOE_KEY_STRIDE) + ra
    kb = r[:, 1:2] * float(MOE_KEY_STRIDE) + rb
    rank_ref[...] = jnp.where(lane_i == 0, ka, jnp.where(lane_i == 1, kb, 0.0))
    run_ref[0:1, :] = run_ref[0:1, :] + jnp.sum(both, axis=0, keepdims=True)
    cnt_ref[...] = run_ref[...]


def _rank(route, tm=512):
    s = route.shape[0]
    return pl.pallas_call(
        functools.partial(_rank_kernel, tm=tm),
        grid=(s // tm,),
        in_specs=[pl.BlockSpec((tm, LANES), lambda i: (i, 0))],
        out_specs=[pl.BlockSpec((tm, LANES), lambda i: (i, 0)), pl.BlockSpec((8, LANES), lambda i: (0, 0))],
        out_shape=[jax.ShapeDtypeStruct((s, LANES), F32), jax.ShapeDtypeStruct((8, LANES), F32)],
        scratch_shapes=[pltpu.VMEM((8, LANES), F32)],
        compiler_params=_cparams(("arbitrary",)),
    )(route)


def _slot(key_ref, ps_ref, t):
    key = key_ref[t]
    return ps_ref[key >> MOE_KEY_SHIFT] + (key & (MOE_KEY_STRIDE - 1))


def _dispatch_kernel(ka_ref, kb_ref, ps_ref, h_ref, xs_in_ref, xs_ref, sem, *, tt):
    del xs_in_ref
    base = pl.program_id(0) * tt

    rows = TOKEN_WORD_ROWS

    def copy(t, slot):
        return pltpu.make_async_copy(h_ref.at[pl.ds(pl.multiple_of(t * rows, rows), rows)],
                                     xs_ref.at[pl.ds(pl.multiple_of(slot * rows, rows), rows)], sem)

    def issue(t, c):
        copy(t, _slot(ka_ref, ps_ref, base + t)).start()
        copy(t, _slot(kb_ref, ps_ref, base + t)).start()
        return c

    lax.fori_loop(0, tt, issue, 0)

    def drain(t, c):
        copy(0, 0).wait()
        copy(0, 0).wait()
        return c

    lax.fori_loop(0, tt, drain, 0)


def _dispatch(hp, key_a, key_b, starts, n_slots, tt=512):
    rows = TOKEN_WORD_ROWS
    s = hp.shape[0] // rows
    xs0 = jnp.zeros((n_slots * rows, LANES), hp.dtype)
    grid_spec = pltpu.PrefetchScalarGridSpec(
        num_scalar_prefetch=3,
        grid=(s // tt,),
        in_specs=[pl.BlockSpec((tt * rows, LANES), lambda i, ka, kb, ps: (i, 0)),
                  pl.BlockSpec(memory_space=pl.ANY)],
        out_specs=pl.BlockSpec(memory_space=pl.ANY),
        scratch_shapes=[pltpu.SemaphoreType.DMA(())],
    )
    xs = pl.pallas_call(
        functools.partial(_dispatch_kernel, tt=tt),
        grid_spec=grid_spec,
        out_shape=jax.ShapeDtypeStruct(xs0.shape, xs0.dtype),
        input_output_aliases={4: 0},
        compiler_params=_cparams(("arbitrary",)),
    )(key_a, key_b, starts, hp, xs0)
    return xs


def _expert_kernel(be_ref, nu_ref, nb_ref, x_ref, wg_hbm, wu_hbm, wd_hbm, y_ref,
                   wg_buf, wu_buf, wd_buf, wgb_ref, wub_ref, wdb_ref, sem, slot_ref, *, bm):
    b = pl.program_id(0)
    n_used = nu_ref[0]

    def fetch(e, slot):
        return (pltpu.make_async_copy(wg_hbm.at[e], wg_buf.at[slot], sem.at[0, slot]),
                pltpu.make_async_copy(wu_hbm.at[e], wu_buf.at[slot], sem.at[1, slot]),
                pltpu.make_async_copy(wd_hbm.at[e], wd_buf.at[slot], sem.at[2, slot]))

    @pl.when(b == 0)
    def _():
        slot_ref[0] = 0
        for cp in fetch(be_ref[0], 0):
            cp.start()

    @pl.when(b < n_used)
    def _():
        e = be_ref[b]
        first = jnp.logical_or(b == 0, e != be_ref[jnp.maximum(b - 1, 0)])

        @pl.when(first)
        def _():
            slot = slot_ref[0]
            for cp in fetch(e, slot):
                cp.wait()
            nxt = b + nb_ref[e]

            @pl.when(nxt < n_used)
            def _():
                for cp in fetch(be_ref[jnp.minimum(nxt, be_ref.shape[0] - 1)], 1 - slot):
                    cp.start()

            wgb_ref[...] = wg_buf[slot].astype(BF16)
            wub_ref[...] = wu_buf[slot].astype(BF16)
            wdb_ref[...] = wd_buf[slot].astype(BF16)
            slot_ref[0] = 1 - slot

        parts = []
        for i in range(TOKEN_WORD_ROWS):
            word = x_ref[pl.ds(i, bm, stride=TOKEN_WORD_ROWS), :]
            parts.append(lax.bitcast_convert_type(word << 16, F32).astype(BF16))
            parts.append(lax.bitcast_convert_type(word & jnp.uint32(0xFFFF0000), F32).astype(BF16))
        x = jnp.concatenate(parts, axis=1)
        g = jnp.dot(x, wgb_ref[...], preferred_element_type=F32)
        u = jnp.dot(x, wub_ref[...], preferred_element_type=F32)
        mid = (g * _sigmoid(g) * u).astype(BF16)
        y = jnp.dot(mid, wdb_ref[...], preferred_element_type=F32)
        for c in range(TOKEN_F32_ROWS):
            y_ref[pl.ds(c, bm, stride=TOKEN_F32_ROWS), :] = y[:, c * LANES:(c + 1) * LANES]

    @pl.when(b >= nu_ref[0])
    def _():
        y_ref[...] = jnp.zeros_like(y_ref)


def _experts(xs, block_expert, n_used, expert_blocks, w_gate, w_up, w_down, bm=MOE_BLOCK):
    n_slots = xs.shape[0] // TOKEN_WORD_ROWS
    d, ff = w_gate.shape[1], w_gate.shape[2]
    n_blocks = n_slots // bm
    hbm = pl.BlockSpec(memory_space=pl.ANY)
    grid_spec = pltpu.PrefetchScalarGridSpec(
        num_scalar_prefetch=3,
        grid=(n_blocks,),
        in_specs=[pl.BlockSpec((bm * TOKEN_WORD_ROWS, LANES),
                               lambda b, be, nu, nb: (jnp.minimum(b, nu[0] - 1), 0)),
                  hbm, hbm, hbm],
        out_specs=pl.BlockSpec((bm * TOKEN_F32_ROWS, LANES), lambda b, be, nu, nb: (b, 0)),
        scratch_shapes=[
            pltpu.VMEM((2, d, ff), F32), pltpu.VMEM((2, d, ff), F32), pltpu.VMEM((2, ff, d), F32),
            pltpu.VMEM((d, ff), BF16), pltpu.VMEM((d, ff), BF16), pltpu.VMEM((ff, d), BF16),
            pltpu.SemaphoreType.DMA((3, 2)),
            pltpu.SMEM((1,), jnp.int32),
        ],
    )
    return pl.pallas_call(
        functools.partial(_expert_kernel, bm=bm),
        grid_spec=grid_spec,
        out_shape=jax.ShapeDtypeStruct((n_slots * TOKEN_F32_ROWS, LANES), F32),
        compiler_params=_cparams(("arbitrary",)),
    )(block_expert, n_used, expert_blocks, xs, w_gate, w_up, w_down)


def _combine_kernel(ka_ref, kb_ref, ps_ref, x_ref, wa_ref, wb_ref, y_ref, o_ref, bufa, bufb, sem, *, tt):
    base = pl.program_id(0) * tt

    rows = TOKEN_F32_ROWS

    def slab(ref, i):
        return ref.at[pl.ds(pl.multiple_of(i * rows, rows), rows)]

    def copy_a(t, slot):
        return pltpu.make_async_copy(slab(y_ref, slot), slab(bufa, t), sem.at[0])

    def copy_b(t, slot):
        return pltpu.make_async_copy(slab(y_ref, slot), slab(bufb, t), sem.at[1])

    def issue(t, c):
        copy_a(t, _slot(ka_ref, ps_ref, base + t)).start()
        copy_b(t, _slot(kb_ref, ps_ref, base + t)).start()
        return c

    lax.fori_loop(0, tt, issue, 0)

    def drain(t, c):
        copy_a(0, 0).wait()
        copy_b(0, 0).wait()
        return c

    lax.fori_loop(0, tt, drain, 0)
    wa, wb = wa_ref[...], wb_ref[...]
    for c in range(rows):
        ya = bufa[pl.ds(c, tt, stride=rows), :]
        yb = bufb[pl.ds(c, tt, stride=rows), :]
        cols = slice(c * LANES, (c + 1) * LANES)
        o_ref[:, cols] = x_ref[:, cols] + wa * ya + wb * yb


def _combine(x1, w_a, w_b, y, key_a, key_b, starts, tt=256):
    s, d = x1.shape
    rows = TOKEN_F32_ROWS
    tok = pl.BlockSpec((tt, d), lambda i, ka, kb, ps: (i, 0))
    wsp = pl.BlockSpec((tt, LANES), lambda i, ka, kb, ps: (i, 0))
    grid_spec = pltpu.PrefetchScalarGridSpec(
        num_scalar_prefetch=3,
        grid=(s // tt,),
        in_specs=[tok, wsp, wsp, pl.BlockSpec(memory_space=pl.ANY)],
        out_specs=tok,
        scratch_shapes=[pltpu.VMEM((tt * rows, LANES), F32), pltpu.VMEM((tt * rows, LANES), F32),
                        pltpu.SemaphoreType.DMA((2,))],
    )
    return pl.pallas_call(
        functools.partial(_combine_kernel, tt=tt),
        grid_spec=grid_spec,
        out_shape=jax.ShapeDtypeStruct((s, d), F32),
        compiler_params=_cparams(("arbitrary",)),
    )(key_a, key_b, starts, x1, w_a, w_b, y)


def _moe(x1, gain, w_gr, b_gr, w_er, b_er, w_gate, w_up, w_down):
    s, d = x1.shape
    pad = LANES - MOE_GROUPS - MOE_EXPERTS
    w_router = jnp.concatenate([w_gr, w_er, jnp.zeros((d, pad), F32)], axis=1)
    b_router = jnp.concatenate([b_gr, b_er, jnp.zeros((pad,), F32)]).reshape(1, LANES)
    h2, route, w_a, w_b = _router(x1, gain, w_router, b_router)
    rank, cnt = _rank(route)
    counts = cnt[0, :MOE_EXPERTS].astype(jnp.int32)
    padded = (counts + MOE_BLOCK - 1) // MOE_BLOCK * MOE_BLOCK
    padded_ends = jnp.cumsum(padded)
    padded_starts = padded_ends - padded
    assert 2 * s <= MOE_KEY_STRIDE
    key_a = rank[:, 0].astype(jnp.int32)
    key_b = rank[:, 1].astype(jnp.int32)
    starts = padded_starts.astype(jnp.int32)
    n_blocks = -(-(2 * s) // MOE_BLOCK) + MOE_EXPERTS
    n_slots = n_blocks * MOE_BLOCK
    block_expert = jnp.clip(
        jnp.searchsorted(padded_ends, jnp.arange(n_blocks, dtype=jnp.int32) * MOE_BLOCK, side="right"),
        0, MOE_EXPERTS - 1).astype(jnp.int32)
    n_used = (padded_ends[-1:] // MOE_BLOCK).astype(jnp.int32)
    xs = _dispatch(h2, key_a, key_b, starts, n_slots)
    expert_blocks = (padded // MOE_BLOCK).astype(jnp.int32)
    y = _experts(xs, block_expert, n_used, expert_blocks, w_gate, w_up, w_down)
    return _combine(x1, w_a, w_b, y, key_a, key_b, starts)


def _alibi_slopes():
    n = ATT_GROUPS * ATT_HEADS
    s = jnp.exp2(-ALIBI_MAX_BIAS * jnp.arange(1, n + 1, dtype=F32) / n)
    return s.reshape(ATT_GROUPS, ATT_HEADS)


def _layer(x, norm1_gain, w_in, q_norm_gain, k_norm_gain, gdn_conv_w, gdn_a_log, gdn_dt_bias,
           gdn_norm_gain, w_branch_att, w_branch_gdn, w_out, norm2_gain, w_group_router,
           b_group_router, w_expert_router, b_expert_router, w_gate, w_up, w_down):
    s, d = x.shape
    h = _rmsnorm(x, norm1_gain)

    qg = jnp.broadcast_to(q_norm_gain[:, None, :] * (HEAD_DIM ** -0.5), (ATT_GROUPS, ATT_HEADS, HEAD_DIM))
    kg = jnp.broadcast_to(k_norm_gain[:, None, :], (ATT_GROUPS, ATT_HEADS, HEAD_DIM))
    qk_gain = jnp.concatenate([qg.reshape(-1), kg.reshape(-1)]).reshape(1, QK_COLS)

    tn = 1024
    dils = [dil for _, dil in ATT_PATTERNS]
    v_att = _inproj(h, w_in, lambda j: QK_COLS // tn + j, ATT_W // tn, "heads", dils=dils)
    v_att = [v.reshape(ATT_HEADS, dil, s // dil, HEAD_DIM) for v, dil in zip(v_att, dils)]
    heads = _inproj(h, w_in, lambda j: (QK_COLS + ATT_W) // tn + j, (HEADS_COLS - ATT_W) // tn, "heads")
    ab = _inproj(h, w_in, lambda j: AB_OFF // AB_COLS + j, 1, "f32", tn=AB_COLS)
    gates = _inproj(h, w_in[:, GATE_OFF:], lambda j: j, GATE_COLS // tn, "sigmoid")

    slopes = _alibi_slopes()
    outs, lses = [], []
    for g, (window, dil) in enumerate(ATT_PATTERNS):
        blk = lambda j, g=g: 2 * g + j + jnp.where(j >= 2, ATT_GROUPS * ATT_W // tn - 2, 0)
        qk = _inproj(h, w_in, blk, 4, "qknorm", gain=qk_gain, gain_block=blk, dils=(dil,))
        qk = qk.reshape(2 * ATT_HEADS, dil, s // dil, HEAD_DIM)
        o, lse = _attn_group(qk, v_att[g], slopes[g], window, dil)
        outs.append(o)
        lses.append(lse)
    y_att = _attn_combine(outs, lses, dils)

    gq = _gdn_conv(heads, gdn_conv_w, 0)
    prm = jnp.concatenate([gdn_a_log.reshape(-1), gdn_dt_bias.reshape(-1)]).reshape(1, LANES)
    z_first = 2 * GDN_QK_HEADS + GDN_V_HEADS
    y_gdn = _gdn(gq, heads, z_first, ab, prm, gdn_norm_gain.reshape(1, HEAD_DIM))

    merged = _branch(y_att, y_gdn, w_branch_att, w_branch_gdn, gates)
    x1 = _outproj(merged, w_out, x)
    return _moe(x1, norm2_gain, w_group_router, b_group_router, w_expert_router, b_expert_router,
                w_gate, w_up, w_down)


def kernel(x, norm1_gain, w_in, q_norm_gain, k_norm_gain, gdn_conv_w, gdn_a_log, gdn_dt_bias,
           gdn_norm_gain, w_branch_att, w_branch_gdn, w_out, norm2_gain, w_group_router,
           b_group_router, w_expert_router, b_expert_router, w_gate, w_up, w_down):
    b, s, d = x.shape
    params = (norm1_gain, w_in, q_norm_gain, k_norm_gain, gdn_conv_w, gdn_a_log, gdn_dt_bias,
              gdn_norm_gain, w_branch_att, w_branch_gdn, w_out, norm2_gain, w_group_router,
              b_group_router, w_expert_router, b_expert_router, w_gate, w_up, w_down)
    outs = []
    for bi in range(b):
        xb = x[bi]
        for i in range(norm1_gain.shape[0]):
            xb = _layer(xb, *(p[i] for p in params))
        outs.append(xb)
    return jnp.stack(outs, axis=0)
```
